```python
import jax
import jax.numpy as jnp
from jax import lax
import numpy as np


D_MODEL = 1024
BATCH = 4
SEQ = 8192
DEPTH = 2

GRID_W = 64
CTX_LEN = 256

D_A = D_MODEL // 4
CONV_A = 3
H_B = 4
DK_B = D_MODEL // 16
DV_B = D_MODEL // 16
D_B = H_B * DV_B
C_GROUPS = 4
C_GDIM = D_MODEL // 16
D_C = C_GROUPS * C_GDIM
H_D = 4
P_D = D_MODEL // 16
D_D = H_D * P_D
G_D = 2
N_D = D_MODEL // 16
CONV_D = 3
D_XBC = D_D + 2 * G_D * N_D

N_BRANCH = 4
CHUNK = 64
IN_SIZES = (D_A, D_A, D_A, H_B * DK_B, 2 * H_B * DK_B, D_B, D_B, D_C, D_D, D_XBC, 2 * H_D, N_BRANCH * D_MODEL)
N_IN = sum(IN_SIZES)

N_GROUPS = 4
EXPERTS_PER_GROUP = 8
N_EXPERTS = N_GROUPS * EXPERTS_PER_GROUP
TOP_K = 2
D_FF = D_MODEL // 2
MOE_BLOCK = 128

EPS = 1e-6

kernel_name = 'hybrid_prefix_flow_block'


def rms_norm(x, w):
    xf = x.astype(jnp.float32)
    y = xf * lax.rsqrt(jnp.mean(xf * xf, axis=-1, keepdims=True) + EPS)
    return (y * w.astype(jnp.float32)).astype(x.dtype)


def modulate(h, shift, scale):
    return h * (1 + scale) + shift


def depthwise_conv(x, w, b):
    k = w.shape[0]
    y = lax.conv_general_dilated(x, w[:, None, :].astype(x.dtype), window_strides=(1,),
                                 padding=((k // 2, k // 2),), dimension_numbers=('NWC', 'WIO', 'NWC'),
                                 feature_group_count=x.shape[-1])
    return y + b.astype(x.dtype)


def chunk_scan(q, k, v, log_f, s0):
    bsz, nh, t_len, _ = q.shape
    n_chunks = t_len // CHUNK

    def to_chunks(a):
        return jnp.moveaxis(a.reshape(bsz, nh, n_chunks, CHUNK, a.shape[-1]), 2, 0)

    incl = jnp.tril(jnp.ones((CHUNK, CHUNK), dtype=bool))[:, :, None]

    def step(state, inp):
        qc, kc, vc, gc = inp
        cum = jnp.cumsum(gc, axis=-2)
        rel = cum[..., :, None, :] - cum[..., None, :, :]
        decay = jnp.exp(jnp.where(incl, rel, -jnp.inf))
        if gc.shape[-1] == 1:
            scores = jnp.einsum('bhtd,bhsd->bhts', qc, kc) * decay[..., 0]
        else:
            scores = jnp.einsum('bhtd,bhsd,bhtsd->bhts', qc, kc, decay)
        out = (jnp.einsum('bhts,bhse->bhte', scores, vc)
               + jnp.einsum('bhtd,bhde->bhte', qc * jnp.exp(cum), state))
        last = cum[..., -1:, :]
        state = (jnp.exp(last[..., 0, :])[..., None] * state
                 + jnp.einsum('bhsd,bhse->bhde', kc * jnp.exp(last - cum), vc))
        return state, out

    state, outs = lax.scan(step, s0, (to_chunks(q), to_chunks(k), to_chunks(v), to_chunks(log_f)))
    return jnp.moveaxis(outs, 0, 2).reshape(bsz, nh, t_len, v.shape[-1]), state


def to_heads(a, n_heads):
    bsz, t_len, _ = a.shape
    return a.reshape(bsz, t_len, n_heads, -1).transpose(0, 2, 1, 3).astype(jnp.float32)


def flip_t(a):
    return jnp.flip(a, axis=2)


def hgrn_log_forget(f_raw, lb):
    lb = lb.reshape(H_B, 1, DK_B)
    return jnp.logaddexp(jnp.log(lb), jnp.log1p(-lb) + jax.nn.log_sigmoid(f_raw))


def token_mixer(h, lp, lb, init, need_out):
    bsz, t_len, _ = h.shape
    split_at = np.cumsum(IN_SIZES)[:-1].tolist()
    (a_b, a_c, a_x, q_raw, f_raw, i_raw, g_raw, four, z, xbc, dt_raw, gate_raw) = jnp.split(
        h @ lp['w_in'], split_at, axis=-1)
    if init is None:
        zb = jnp.zeros((bsz, H_B, DK_B, DV_B), jnp.float32)
        zd = jnp.zeros((bsz, H_D, N_D, P_D), jnp.float32)
        init = (zb, zb, zd, zd)
    s_hf, s_hb, s_mf, s_mb = init

    y_a = a_b * depthwise_conv(a_c * a_x, lp['conv_a_w'], lp['conv_a_b'])

    q = to_heads(jax.nn.silu(q_raw), H_B)
    v = to_heads(i_raw, H_B)
    ff_raw, fb_raw = jnp.split(f_raw, 2, axis=-1)
    log_ff = hgrn_log_forget(to_heads(ff_raw, H_B), lb[0])
    log_fb = hgrn_log_forget(to_heads(fb_raw, H_B), lb[1])
    o_f, s_hf = chunk_scan(q, -jnp.expm1(log_ff), v, log_ff, s_hf)
    o_b, s_hb = chunk_scan(flip_t(q), flip_t(-jnp.expm1(log_fb)), flip_t(v), flip_t(log_fb), s_hb)
    o = (o_f + flip_t(o_b)).transpose(0, 2, 1, 3)
    y_b = (rms_norm(o, lp['hgrn_norm_w'])
           * jax.nn.silu(g_raw.reshape(bsz, t_len, H_B, DV_B).astype(jnp.float32)))
    y_b = y_b.reshape(bsz, t_len, D_B).astype(h.dtype)

    y_c = jnp.fft.fft2(four.reshape(bsz, t_len, C_GROUPS, C_GDIM).astype(jnp.float32),
                       axes=(1, 3), norm='ortho').real
    y_c = y_c.reshape(bsz, t_len, D_C).astype(h.dtype)

    xbc = jax.nn.silu(depthwise_conv(xbc, lp['ssm_conv_w'], lp['ssm_conv_b']))
    xs, b_in, c_in = jnp.split(xbc, [D_D, D_D + G_D * N_D], axis=-1)
    xh = to_heads(xs, H_D)

    def group_heads(a):
        a = jnp.repeat(a.reshape(bsz, t_len, G_D, N_D), H_D // G_D, axis=2)
        return a.transpose(0, 2, 1, 3).astype(jnp.float32)

    bh, ch = group_heads(b_in), group_heads(c_in)
    dt = jax.nn.softplus(dt_raw.reshape(bsz, t_len, 2, H_D).astype(jnp.float32)
                         + lp['ssm_dt_bias'].astype(jnp.float32)).transpose(2, 0, 3, 1)
    log_a = dt * (-jnp.exp(lp['ssm_A_log'].astype(jnp.float32)))[:, None, :, None]
    y_f, s_mf = chunk_scan(ch, bh * dt[0][..., None], xh, log_a[0][..., None], s_mf)
    y_r, s_mb = chunk_scan(flip_t(ch), flip_t(bh * dt[1][..., None]), flip_t(xh),
                           flip_t(log_a[1][..., None]), s_mb)
    y = y_f + flip_t(y_r) + lp['ssm_D'].astype(jnp.float32)[:, None, None] * xh
    y = (y.transpose(0, 2, 1, 3).reshape(bsz, t_len, G_D, D_D // G_D)
         * jax.nn.silu(z.astype(jnp.float32)).reshape(bsz, t_len, G_D, D_D // G_D))
    y_d = rms_norm(y, lp['ssm_norm_w'].reshape(G_D, D_D // G_D)).reshape(bsz, t_len, D_D).astype(h.dtype)

    states = (s_hf, s_hb, s_mf, s_mb)
    if not need_out:
        return None, states

    gates = jax.nn.sigmoid(gate_raw.reshape(bsz, t_len, N_BRANCH, D_MODEL))
    branches = (y_a, y_b, y_c, y_d)
    merged = gates[:, :, 0] * (branches[0] @ lp['w_branch'][0])
    for kb in range(1, N_BRANCH):
        merged = merged + gates[:, :, kb] * (branches[kb] @ lp['w_branch'][kb])
    return merged @ lp['w_out'], states


def grouped_experts(h, expert_ids, weights, w1, w3, w2):
    n_tok, d = h.shape
    n_assign = expert_ids.shape[0]
    token_ids = jnp.arange(n_assign, dtype=jnp.int32) // (n_assign // n_tok)
    order = jnp.argsort(expert_ids)
    e_sorted = expert_ids[order]
    tok_sorted = token_ids[order]
    counts = jnp.bincount(expert_ids, length=N_EXPERTS)
    starts = jnp.cumsum(counts) - counts
    padded = (counts + MOE_BLOCK - 1) // MOE_BLOCK * MOE_BLOCK
    pad_ends = jnp.cumsum(padded)
    pad_starts = pad_ends - padded
    dest = pad_starts[e_sorted] + (jnp.arange(n_assign, dtype=jnp.int32) - starts[e_sorted])
    n_blocks = -(-n_assign // MOE_BLOCK) + N_EXPERTS
    buf = jnp.zeros((n_blocks * MOE_BLOCK, d), h.dtype).at[dest].set(h[tok_sorted])
    block_start = jnp.arange(n_blocks, dtype=jnp.int32) * MOE_BLOCK
    block_expert = jnp.minimum(jnp.sum(block_start[:, None] >= pad_ends[None, :], axis=-1), N_EXPERTS - 1)

    def expert_block(args):
        xb, e = args
        return (jax.nn.silu(xb @ w1[e]) * (xb @ w3[e])) @ w2[e]

    out = lax.map(expert_block, (buf.reshape(n_blocks, MOE_BLOCK, d), block_expert)).reshape(-1, d)
    y_assign = out[dest].astype(jnp.float32) * weights[order][:, None]
    return jax.ops.segment_sum(y_assign, tok_sorted, num_segments=n_tok).astype(h.dtype)


def hier_moe(h, rg_w, rg_b, re_w, re_b, w1, w3, w2):
    n_tok = h.shape[0]
    hf = h.astype(jnp.float32)
    g_logits = hf @ rg_w.astype(jnp.float32) + rg_b.astype(jnp.float32)
    g_prob = jax.nn.softmax(g_logits, axis=-1)
    g_top = jnp.argmax(g_logits, axis=-1).astype(jnp.int32)
    p_group = jnp.take_along_axis(g_prob, g_top[:, None], axis=-1)
    e_logits = (hf @ re_w.astype(jnp.float32) + re_b.astype(jnp.float32)).reshape(n_tok, N_GROUPS, EXPERTS_PER_GROUP)
    e_in = jnp.take_along_axis(e_logits, g_top[:, None, None], axis=1)[:, 0]
    top_v, top_i = lax.top_k(e_in, TOP_K)
    w = p_group * jax.nn.softmax(top_v, axis=-1)
    expert = g_top[:, None] * EXPERTS_PER_GROUP + top_i.astype(jnp.int32)
    return grouped_experts(h, expert.reshape(-1), w.reshape(-1), w1, w3, w2)


def setup_inputs(seed: int = 0) -> dict:
    key = jax.random.key(seed)
    ks = jax.random.split(key, 32)
    f32 = jnp.float32
    nrm = lambda k, s, sc: jax.random.normal(k, s, f32) * sc
    dt0 = jnp.exp(jax.random.uniform(ks[14], (DEPTH, 2, H_D), f32) * (jnp.log(0.1) - jnp.log(0.001)) + jnp.log(0.001))
    return {
        'x': nrm(ks[0], (BATCH, SEQ, D_MODEL), 1.0),
        'c': nrm(ks[1], (BATCH, D_MODEL), 1.0),
        'ctx': nrm(ks[2], (BATCH, CTX_LEN, D_MODEL), 1.0),
        'c_ctx': nrm(ks[3], (D_MODEL,), 1.0),
        'ada_w': nrm(ks[4], (DEPTH, D_MODEL, 6 * D_MODEL), 0.5 * D_MODEL ** -0.5),
        'ada_b': nrm(ks[5], (DEPTH, 6 * D_MODEL), 0.01),
        'norm_mix_w': 1.0 + nrm(ks[6], (DEPTH, D_MODEL), 0.01),
        'norm_ffn_w': 1.0 + nrm(ks[7], (DEPTH, D_MODEL), 0.01),
        'w_in': nrm(ks[8], (DEPTH, D_MODEL, N_IN), D_MODEL ** -0.5),
        'conv_a_w': nrm(ks[9], (DEPTH, CONV_A, D_A), CONV_A ** -0.5),
        'conv_a_b': nrm(ks[10], (DEPTH, D_A), 0.01),
        'hgrn_lb': nrm(ks[11], (2, DEPTH, H_B * DK_B), 1.0),
        'hgrn_norm_w': 1.0 + nrm(ks[12], (DEPTH, DV_B), 0.01),
        'ssm_conv_w': nrm(ks[13], (DEPTH, CONV_D, D_XBC), CONV_D ** -0.5),
        'ssm_conv_b': nrm(ks[15], (DEPTH, D_XBC), 0.01),
        'ssm_A_log': jnp.log(jax.random.uniform(ks[16], (DEPTH, 2, H_D), f32, 1.0, 16.0)),
        'ssm_dt_bias': dt0 + jnp.log(-jnp.expm1(-dt0)),
        'ssm_D': 1.0 + nrm(ks[17], (DEPTH, H_D), 0.01),
        'ssm_norm_w': 1.0 + nrm(ks[18], (DEPTH, D_D), 0.01),
        'w_branch': nrm(ks[19], (DEPTH, N_BRANCH, D_A, D_MODEL), D_A ** -0.5),
        'w_out': nrm(ks[20], (DEPTH, D_MODEL, D_MODEL), D_MODEL ** -0.5),
        'router_group_w': nrm(ks[21], (DEPTH, D_MODEL, N_GROUPS), D_MODEL ** -0.5),
        'router_group_b': nrm(ks[22], (DEPTH, N_GROUPS), 0.01),
        'router_expert_w': nrm(ks[23], (DEPTH, D_MODEL, N_EXPERTS), D_MODEL ** -0.5),
        'router_expert_b': nrm(ks[24], (DEPTH, N_EXPERTS), 0.01),
        'moe_w1': nrm(ks[25], (DEPTH, N_EXPERTS, D_MODEL, D_FF), D_MODEL ** -0.5),
        'moe_w3': nrm(ks[26], (DEPTH, N_EXPERTS, D_MODEL, D_FF), D_MODEL ** -0.5),
        'moe_w2': nrm(ks[27], (DEPTH, N_EXPERTS, D_FF, D_MODEL), D_FF ** -0.5),
        'final_norm_w': 1.0 + nrm(ks[28], (D_MODEL,), 0.01),
    }


def reference(x, c, ctx, c_ctx, ada_w, ada_b, norm_mix_w, norm_ffn_w, w_in, conv_a_w, conv_a_b,
              hgrn_lb, hgrn_norm_w, ssm_conv_w, ssm_conv_b, ssm_A_log, ssm_dt_bias, ssm_D, ssm_norm_w,
              w_branch, w_out, router_group_w, router_group_b, router_expert_w, router_expert_b,
              moe_w1, moe_w3, moe_w2, final_norm_w):
    lb_all = jnp.cumsum(jax.nn.softmax(hgrn_lb.astype(jnp.float32), axis=1), axis=1)
    lb_all = lb_all - lb_all[:, :1]
    s_c = jax.nn.silu(c)
    s_ctx = jax.nn.silu(c_ctx)
    for l in range(DEPTH):
        last = l == DEPTH - 1
        lp = {'w_in': w_in[l], 'conv_a_w': conv_a_w[l], 'conv_a_b': conv_a_b[l],
              'hgrn_norm_w': hgrn_norm_w[l], 'ssm_conv_w': ssm_conv_w[l], 'ssm_conv_b': ssm_conv_b[l],
              'ssm_A_log': ssm_A_log[l], 'ssm_dt_bias': ssm_dt_bias[l], 'ssm_D': ssm_D[l],
              'ssm_norm_w': ssm_norm_w[l], 'w_branch': w_branch[l], 'w_out': w_out[l]}
        m_l = jnp.split((s_c @ ada_w[l] + ada_b[l])[:, None, :], 6, axis=-1)
        m_c = jnp.split(s_ctx @ ada_w[l] + ada_b[l], 6, axis=-1)

        h_c = modulate(rms_norm(ctx, norm_mix_w[l]), m_c[0], m_c[1])
        mix_c, ctx_states = token_mixer(h_c, lp, lb_all[:, l], None, not last)
        h_l = modulate(rms_norm(x, norm_mix_w[l]), m_l[0], m_l[1])
        mix_l, _ = token_mixer(h_l, lp, lb_all[:, l], ctx_states, True)
        x = x + m_l[2] * mix_l

        moe_args = (router_group_w[l], router_group_b[l], router_expert_w[l], router_expert_b[l],
                    moe_w1[l], moe_w3[l], moe_w2[l])
        f_l = modulate(rms_norm(x, norm_ffn_w[l]), m_l[3], m_l[4]).reshape(-1, D_MODEL)
        if last:
            y_l = hier_moe(f_l, *moe_args)
        else:
            ctx = ctx + m_c[2] * mix_c
            f_c = modulate(rms_norm(ctx, norm_ffn_w[l]), m_c[3], m_c[4]).reshape(-1, D_MODEL)
            y_all = hier_moe(jnp.concatenate([f_l, f_c], axis=0), *moe_args)
            n_lat = f_l.shape[0]
            y_l = y_all[:n_lat]
            ctx = ctx + m_c[5] * y_all[n_lat:].reshape(ctx.shape)
        x = x + m_l[5] * y_l.reshape(x.shape)
    return rms_norm(x, final_norm_w)
```

```python
import functools

import numpy as np
import jax
import jax.numpy as jnp
from jax import lax
from jax.experimental import pallas as pl
from jax.experimental.pallas import tpu as pltpu

F32, BF16, I32 = jnp.float32, jnp.bfloat16, jnp.int32
HIGHEST = lax.Precision.HIGHEST
EPS = 1e-6

D_MODEL = 1024
SLAB = 256
N_HEADS = 4
HEAD = 64
C_GDIM = 64
N_BRANCH = 4
CHUNK = 64
N_LEVELS = 6
N_GROUPS, EXPERTS_PER_GROUP = 4, 8
N_EXPERTS = N_GROUPS * EXPERTS_PER_GROUP
D_FF = 512
ROUTER_ROWS = 8 + N_EXPERTS
MOE_ROWS = 256
VMEM_LIMIT = 56 * 1024 * 1024

S_YA, S_Q, S_FF, S_FB, S_I, S_G, S_FOUR, S_Z, S_XS, S_BX, S_CX, S_DTF, S_DTB = range(13)
N_SLABS = 13


def _sigmoid(x):
    return 1.0 / (1.0 + jnp.exp(-x))


def _softplus(x):
    return jnp.maximum(x, 0.0) + jnp.log1p(jnp.exp(-jnp.abs(x)))


def _tile(n, pref):
    t = pref
    while n % t:
        t //= 2
    return t


def _params(*sem):
    return pltpu.CompilerParams(dimension_semantics=sem, vmem_limit_bytes=VMEM_LIMIT)


def _nt(a, b, **kw):
    return lax.dot_general(a, b, (((1,), (1,)), ((), ())), preferred_element_type=F32, **kw)


def _tn(a, b, **kw):
    return lax.dot_general(a, b, (((0,), (0,)), ((), ())), preferred_element_type=F32, **kw)


def _ada_body(s_ref, w_ref, b_ref, o_ref):
    s = s_ref[...]
    s = s * _sigmoid(s)
    o_ref[...] = jnp.dot(s, w_ref[...], precision=HIGHEST, preferred_element_type=F32) + b_ref[...]


def _ada(cond, ada_w, ada_b):
    depth = ada_w.shape[0]
    n6 = ada_w.shape[2] // D_MODEL
    return pl.pallas_call(
        _ada_body,
        grid=(depth, n6),
        in_specs=[pl.BlockSpec((8, D_MODEL), lambda l, j: (0, 0)),
                  pl.BlockSpec((None, D_MODEL, D_MODEL), lambda l, j: (l, 0, j)),
                  pl.BlockSpec((None, 1, D_MODEL), lambda l, j: (l, 0, j))],
        out_specs=pl.BlockSpec((None, 8, D_MODEL), lambda l, j: (l, 0, j)),
        out_shape=jax.ShapeDtypeStruct((depth, 8, n6 * D_MODEL), F32),
        compiler_params=_params("parallel", "parallel"),
        name="ada_mod",
    )(cond, ada_w, ada_b.reshape(depth, 1, -1))


def _norm_mod(x, nw, shift, scale):
    ms = jnp.mean(x * x, axis=-1, keepdims=True)
    return (x * lax.rsqrt(ms + EPS) * nw) * (1.0 + scale) + shift


def _inproj_body(x_ref, xp_ref, xn_ref, mod_ref, nw_ref, w_ref, caw_ref, cab_ref, scw_ref, scb_ref,
                 p_ref, *, tm, nt):
    i = pl.program_id(1)
    xe = jnp.concatenate([xp_ref[...], x_ref[...], xn_ref[...]], axis=0)
    h = _norm_mod(xe, nw_ref[...], mod_ref[0:1, :], mod_ref[1:2, :])
    pr = jnp.dot(h.astype(BF16), w_ref[...], preferred_element_type=F32)
    rows = lax.broadcasted_iota(I32, (tm + 16, 1), 0)
    lo = jnp.where(i > 0, 0, 8)
    hi = jnp.where(i < nt - 1, tm + 16, tm + 8)
    vm = jnp.where((rows >= lo) & (rows < hi), 1.0, 0.0)

    def conv3(u, w, b):
        u = u * vm
        return u[7:tm + 7] * w[0:1, :] + u[8:tm + 8] * w[1:2, :] + u[9:tm + 9] * w[2:3, :] + b[...]

    ca = conv3(pr[:, SLAB:2 * SLAB] * pr[:, 2 * SLAB:3 * SLAB], caw_ref, cab_ref)
    p_ref[S_YA] = pr[8:tm + 8, 0:SLAB] * ca
    for s in range(7):
        p_ref[S_Q + s] = pr[8:tm + 8, (3 + s) * SLAB:(4 + s) * SLAB]
    cs = conv3(pr[:, 10 * SLAB:13 * SLAB], scw_ref, scb_ref)
    cs = cs * _sigmoid(cs)
    for s in range(3):
        p_ref[S_XS + s] = cs[:, s * SLAB:(s + 1) * SLAB]
    p_ref[S_DTF] = pr[8:tm + 8, 13 * SLAB:14 * SLAB]
    p_ref[S_DTB] = pr[8:tm + 8, 14 * SLAB:15 * SLAB]


def _inproj(x, mod, nw, w_ext, caw, cab, scw, scb):
    bsz, t_len, _ = x.shape
    tm = min(512, t_len)
    nt = t_len // tm
    t8 = tm // 8
    full = lambda shape: pl.BlockSpec(shape, lambda b, i: (0,) * len(shape))
    return pl.pallas_call(
        functools.partial(_inproj_body, tm=tm, nt=nt),
        grid=(bsz, nt),
        in_specs=[pl.BlockSpec((None, tm, D_MODEL), lambda b, i: (b, i, 0)),
                  pl.BlockSpec((None, 8, D_MODEL), lambda b, i: (b, jnp.maximum(i * t8 - 1, 0), 0)),
                  pl.BlockSpec((None, 8, D_MODEL), lambda b, i: (b, jnp.minimum((i + 1) * t8, t_len // 8 - 1), 0)),
                  pl.BlockSpec((None, 8, D_MODEL), lambda b, i: (b, 0, 0)),
                  full((1, D_MODEL)), full(w_ext.shape), full(caw.shape), full(cab.shape),
                  full(scw.shape), full(scb.shape)],
        out_specs=pl.BlockSpec((N_SLABS, None, tm, SLAB), lambda b, i: (0, b, i, 0)),
        out_shape=jax.ShapeDtypeStruct((N_SLABS, bsz, t_len, SLAB), F32),
        compiler_params=_params("parallel", "parallel"),
        name="in_proj",
    )(x, x, x, mod, nw, w_ext, caw, cab, scw, scb)


def _scan_constants():
    t = np.arange(CHUNK)[:, None]
    r = np.arange(CHUNK)[None, :]
    out = {}
    for name, fwd in (("f", True), ("b", False)):
        blocks = [(r <= t) if fwd else (r >= t), (r > t) if fwd else (r < t)]
        qs, ks, masks = [], [], []
        for lvl in range(N_LEVELS):
            m = 1 << lvl
            blk = t // (2 * m)
            ref = blk * 2 * m + m - 1
            upper = (t % (2 * m)) >= m
            s_blk = (r // (2 * m))
            s_upper = (r % (2 * m)) >= m
            if fwd:
                qs.append(upper & (r > ref) & (r <= t))
                ks.append((~upper) & (r > t) & (r <= ref))
                masks.append(upper & (~s_upper) & (s_blk == blk))
            else:
                qs.append((~upper) & (r >= t) & (r <= ref))
                ks.append(upper & (r > ref) & (r < t))
                masks.append((~upper) & s_upper & (s_blk == blk))
        out["cm_" + name] = np.concatenate(blocks + qs + ks, axis=0).astype(np.float32)
        out["lm_" + name] = np.stack([np.tile(mk, (N_HEADS, 1)) for mk in masks]).astype(np.float32)
    lane_head = np.arange(SLAB) // HEAD
    out["hm"] = (lane_head[None, :] == np.arange(8)[:, None]).astype(np.float32)
    out["ebd"] = (lane_head[:, None] == lane_head[None, :]).astype(np.float32)
    return out


def _lin_attn_chunk(qv, kv, vv, logf, s_ref, k, cm, lm_ref, hm_ref, ebd, fwd):
    hi = logf.astype(BF16)
    lo = (logf - hi.astype(F32)).astype(BF16)
    ex = jnp.dot(cm, jnp.concatenate([hi, lo], axis=1), preferred_element_type=F32)
    e = jnp.exp(ex[:, :SLAB] + ex[:, SLAB:])
    c = CHUNK
    p = jnp.zeros((N_HEADS * c, c), F32)
    for lvl in range(N_LEVELS):
        qh = qv * e[(2 + lvl) * c:(3 + lvl) * c]
        kh = (kv * e[(2 + N_LEVELS + lvl) * c:(3 + N_LEVELS + lvl) * c]).astype(BF16)
        qs = jnp.concatenate([qh * hm_ref[h:h + 1, :] for h in range(N_HEADS)], axis=0).astype(BF16)
        p = p + _nt(qs, kh) * lm_ref[lvl]
    state = s_ref[k]
    vb = vv.astype(BF16)
    o = jnp.dot((qv * kv).astype(BF16), ebd, preferred_element_type=F32) * vv
    rv = jnp.dot(p.astype(BF16), vb, preferred_element_type=F32)
    for h in range(N_HEADS):
        o = o + rv[h * c:(h + 1) * c] * hm_ref[h:h + 1, :]
    o = o + _nt((qv * e[0:c]).astype(BF16), state.astype(BF16))
    upd = _tn(vb, (kv * e[c:2 * c]).astype(BF16))
    tot = e[c - 1:c] if fwd else e[0:1]
    s_ref[k] = state * tot + upd * ebd.astype(F32)
    return o


def _scan_body(qf, ff, vf, xsf, bxf, cxf, dtf, qb, fb, vb, xsb, bxb, cxb, dtb,
               lb_ref, alog_ref, dtbias_ref, cmf_ref, cmb_ref, lmf_ref, lmb_ref, hm_ref, ebd_ref, s0_ref,
               ohf, ohb, osf, osb, sout_ref, s_ref, *, layer, nck, nsteps):
    i = pl.program_id(1)

    @pl.when(i == 0)
    def _():
        s_ref[...] = s0_ref[...]

    ebd = ebd_ref[...]
    cmf, cmb = cmf_ref[...], cmb_ref[...]

    def lower_bound(d):
        rows = lb_ref[d]
        ex = jnp.exp(rows - jnp.max(rows, axis=0, keepdims=True))
        prob = ex / jnp.sum(ex, axis=0, keepdims=True)
        lb = jnp.zeros((1, SLAB), F32)
        for j in range(1, layer + 1):
            lb = lb + prob[j:j + 1]
        return lb

    lbs = [lower_bound(0), lower_bound(1)]

    def hgrn_inputs(q_raw, f_raw, lb):
        q = q_raw * _sigmoid(q_raw)
        log_sig = jnp.minimum(f_raw, 0.0) - jnp.log1p(jnp.exp(-jnp.abs(f_raw)))
        a = jnp.log(lb)
        b = jnp.log1p(-lb) + log_sig
        logf = jnp.maximum(a, b) + jnp.log1p(jnp.exp(-jnp.abs(a - b)))
        kk = (1.0 - lb) * _sigmoid(-f_raw)
        return q, kk, logf

    def ssd_inputs(bx, dt_raw, d):
        dt = _softplus(dt_raw + dtbias_ref[d:d + 1, :])
        return bx * dt, -jnp.exp(alog_ref[d:d + 1, :]) * dt

    def body(j, carry):
        rf = pl.ds(pl.multiple_of(j * CHUNK, CHUNK), CHUNK)
        rb = pl.ds(pl.multiple_of((nck - 1 - j) * CHUNK, CHUNK), CHUNK)
        q, kk, logf = hgrn_inputs(qf[rf, :], ff[rf, :], lbs[0])
        ohf[rf, :] = _lin_attn_chunk(q, kk, vf[rf, :], logf, s_ref, 0, cmf, lmf_ref, hm_ref, ebd, True)
        q, kk, logf = hgrn_inputs(qb[rb, :], fb[rb, :], lbs[1])
        ohb[rb, :] = _lin_attn_chunk(q, kk, vb[rb, :], logf, s_ref, 1, cmb, lmb_ref, hm_ref, ebd, False)
        kk, logf = ssd_inputs(bxf[rf, :], dtf[rf, :], 0)
        osf[rf, :] = _lin_attn_chunk(cxf[rf, :], kk, xsf[rf, :], logf, s_ref, 2, cmf, lmf_ref, hm_ref, ebd, True)
        kk, logf = ssd_inputs(bxb[rb, :], dtb[rb, :], 1)
        osb[rb, :] = _lin_attn_chunk(cxb[rb, :], kk, xsb[rb, :], logf, s_ref, 3, cmb, lmb_ref, hm_ref, ebd, False)
        return carry

    lax.fori_loop(0, nck, body, 0)

    @pl.when(i == nsteps - 1)
    def _():
        sout_ref[...] = s_ref[...]


def _scan(p3, hgrn_lb, alog_lane, dtbias_lane, s0, consts, layer):
    _, bsz, t_len, _ = p3.shape
    cb = min(256, t_len)
    nb = t_len // cb
    slab_f = lambda s: pl.BlockSpec((None, None, cb, SLAB), lambda b, i: (s, b, i, 0))
    slab_b = lambda s: pl.BlockSpec((None, None, cb, SLAB), lambda b, i: (s, b, nb - 1 - i, 0))
    full = lambda a: pl.BlockSpec(a.shape, lambda b, i: (0,) * a.ndim)
    cm_f, cm_b = consts["cm_f"].astype(BF16), consts["cm_b"].astype(BF16)
    small = [hgrn_lb, alog_lane, dtbias_lane, cm_f, cm_b, consts["lm_f"], consts["lm_b"], consts["hm"],
             consts["ebd"].astype(BF16)]
    o_f = pl.BlockSpec((None, cb, SLAB), lambda b, i: (b, i, 0))
    o_b = pl.BlockSpec((None, cb, SLAB), lambda b, i: (b, nb - 1 - i, 0))
    st = pl.BlockSpec((None, 4, SLAB, SLAB), lambda b, i: (b, 0, 0, 0))
    o_shape = jax.ShapeDtypeStruct((bsz, t_len, SLAB), F32)
    return pl.pallas_call(
        functools.partial(_scan_body, layer=layer, nck=cb // CHUNK, nsteps=nb),
        grid=(bsz, nb),
        in_specs=([slab_f(s) for s in (S_Q, S_FF, S_I, S_XS, S_BX, S_CX, S_DTF)]
                  + [slab_b(s) for s in (S_Q, S_FB, S_I, S_XS, S_BX, S_CX, S_DTB)]
                  + [full(a) for a in small] + [st]),
        out_specs=[o_f, o_b, o_f, o_b, st],
        out_shape=[o_shape, o_shape, o_shape, o_shape, jax.ShapeDtypeStruct(s0.shape, F32)],
        scratch_shapes=[pltpu.VMEM((4, SLAB, SLAB), F32)],
        compiler_params=_params("parallel", "arbitrary"),
        name="chunk_scan",
    )(*([p3] * 14), *small, s0)


def _fft_sizes(t_len):
    tf = 64 if t_len >= 4096 else 16
    return tf, t_len // tf


def _fft_constants(t_len):
    tf, ts = _fft_sizes(t_len)
    c = np.arange(SLAB)
    same = (c[:, None] // C_GDIM) == (c[None, :] // C_GDIM)
    ang = 2 * np.pi * ((c[:, None] % C_GDIM) * (c[None, :] % C_GDIM) % C_GDIM) / C_GDIM
    chan = np.concatenate([np.cos(ang) * same, -np.sin(ang) * same], axis=1)
    k2 = np.arange(ts)
    a = 2 * np.pi * (k2[:, None] * k2[None, :] % ts) / ts
    flat = np.block([[np.cos(a), np.sin(a)], [-np.sin(a), np.cos(a)]])
    k1 = np.arange(tf)[None, :, None]
    tfi = np.arange(tf)[None, None, :]
    kk2 = np.arange(ts)[:, None, None]
    th = 2 * np.pi * (((tfi * k1 * ts) + tfi * kk2) % t_len) / t_len
    scale = 1.0 / np.sqrt(t_len * C_GDIM)
    slab = np.concatenate([np.cos(th), np.sin(th)], axis=2) * scale
    return chan.astype(np.float32), flat.astype(np.float32), slab.astype(np.float32)


def _fft_flat_body(x_ref, chan_ref, flat_ref, y_ref, *, ts, groups):
    for g in range(groups):
        xg = x_ref[:, g * SLAB:(g + 1) * SLAB]
        a = jnp.dot(xg, chan_ref[...], precision=HIGHEST, preferred_element_type=F32)
        z = jnp.concatenate([a[:, :SLAB], a[:, SLAB:]], axis=0)
        y = jnp.dot(flat_ref[...], z, precision=HIGHEST, preferred_element_type=F32)
        y_ref[0, :, g * SLAB:(g + 1) * SLAB] = y[:ts]
        y_ref[1, :, g * SLAB:(g + 1) * SLAB] = y[ts:]


def _fft_slab_body(y_ref, m_ref, o_ref, *, kb):
    for j in range(kb):
        z = jnp.concatenate([y_ref[0, j], y_ref[1, j]], axis=0)
        o_ref[j] = jnp.dot(m_ref[j], z, precision=HIGHEST, preferred_element_type=F32)


def _fourier(p3, consts):
    _, bsz, t_len, _ = p3.shape
    tf, ts = _fft_sizes(t_len)
    chan, flat, slab = consts
    groups = min(8, tf)
    lanes = groups * SLAB
    y = pl.pallas_call(
        functools.partial(_fft_flat_body, ts=ts, groups=groups),
        grid=(bsz, tf // groups),
        in_specs=[pl.BlockSpec((None, None, ts, lanes), lambda b, j: (S_FOUR, b, 0, j)),
                  pl.BlockSpec(chan.shape, lambda b, j: (0, 0)),
                  pl.BlockSpec(flat.shape, lambda b, j: (0, 0))],
        out_specs=pl.BlockSpec((None, 2, ts, lanes), lambda b, j: (b, 0, 0, j)),
        out_shape=jax.ShapeDtypeStruct((bsz, 2, ts, tf * SLAB), F32),
        compiler_params=_params("parallel", "parallel"),
        name="fft_flat",
    )(p3.reshape(N_SLABS, bsz, ts, tf * SLAB), chan, flat)
    kb = min(8, ts)
    out = pl.pallas_call(
        functools.partial(_fft_slab_body, kb=kb),
        grid=(bsz, ts // kb),
        in_specs=[pl.BlockSpec((None, 2, kb, tf, SLAB), lambda b, j: (b, 0, j, 0, 0)),
                  pl.BlockSpec((kb, tf, 2 * tf), lambda b, j: (j, 0, 0))],
        out_specs=pl.BlockSpec((None, kb, tf, SLAB), lambda b, j: (b, j, 0, 0)),
        out_shape=jax.ShapeDtypeStruct((bsz, ts, tf, SLAB), F32),
        compiler_params=_params("parallel", "parallel"),
        name="fft_slab",
    )(y.reshape(bsz, 2, ts, tf, SLAB), slab)
    return out.transpose(0, 2, 1, 3).reshape(bsz, t_len, SLAB)


def _merge_body(x_ref, ya_ref, g_ref, z_ref, xs_ref, hf_ref, hb_ref, sf_ref, sb_ref, yc_ref,
                mod_ref, nmix_ref, nffn_ref, wg_ref, wb_ref, wo_ref, hnw_ref, sd_ref, snw_ref,
                e64_ref, e128_ref, wr_ref, br_ref,
                xo_ref, f_ref, ids_ref, wts_ref):
    x = x_ref[...]
    h = _norm_mod(x, nmix_ref[...], mod_ref[0:1, :], mod_ref[1:2, :]).astype(BF16)

    def group_rms(v, e_ref, width):
        ms = jnp.dot((v * v).astype(BF16), e_ref[...], preferred_element_type=F32) * (1.0 / width)
        return v * lax.rsqrt(ms + EPS)

    g = g_ref[...]
    y_b = group_rms(hf_ref[...] + hb_ref[...], e64_ref, HEAD) * hnw_ref[...] * (g * _sigmoid(g))
    z = z_ref[...]
    y_d = (sf_ref[...] + sb_ref[...] + sd_ref[...] * xs_ref[...]) * (z * _sigmoid(z))
    y_d = group_rms(y_d, e128_ref, 2 * HEAD) * snw_ref[...]
    branches = (ya_ref[...], y_b, yc_ref[...], y_d)
    merged = None
    for kb in range(N_BRANCH):
        gate = _sigmoid(jnp.dot(h, wg_ref[:, kb * D_MODEL:(kb + 1) * D_MODEL], preferred_element_type=F32))
        term = gate * jnp.dot(branches[kb].astype(BF16), wb_ref[kb], preferred_element_type=F32)
        merged = term if merged is None else merged + term
    mix = jnp.dot(merged.astype(BF16), wo_ref[...], preferred_element_type=F32)
    xo = x + mod_ref[2:3, :] * mix
    xo_ref[...] = xo
    f = _norm_mod(xo, nffn_ref[...], mod_ref[3:4, :], mod_ref[4:5, :])
    f_ref[...] = f

    lg = _nt(wr_ref[...], f, precision=HIGHEST) + br_ref[...]
    gl = lg[0:N_GROUPS]
    gidx = lax.broadcasted_iota(I32, gl.shape, 0)
    gmax = jnp.max(gl, axis=0, keepdims=True)
    g_top = jnp.min(jnp.where(gl == gmax, gidx, N_GROUPS), axis=0, keepdims=True)
    p_group = 1.0 / jnp.sum(jnp.exp(gl - gmax), axis=0, keepdims=True)
    e_in = jnp.zeros((EXPERTS_PER_GROUP, gl.shape[1]), F32)
    for grp in range(N_GROUPS):
        e_in = jnp.where(g_top == grp, lg[8 + 8 * grp:16 + 8 * grp], e_in)
    eidx = lax.broadcasted_iota(I32, e_in.shape, 0)
    v1 = jnp.max(e_in, axis=0, keepdims=True)
    i1 = jnp.min(jnp.where(e_in == v1, eidx, EXPERTS_PER_GROUP), axis=0, keepdims=True)
    rest = jnp.where(eidx == i1, -jnp.inf, e_in)
    v2 = jnp.max(rest, axis=0, keepdims=True)
    i2 = jnp.min(jnp.where(rest == v2, eidx, EXPERTS_PER_GROUP), axis=0, keepdims=True)
    w1 = 1.0 / (1.0 + jnp.exp(v2 - v1))
    row = lax.broadcasted_iota(I32, (8, gl.shape[1]), 0)
    base = g_top * EXPERTS_PER_GROUP
    ids_ref[...] = jnp.where(row == 0, base + i1, jnp.where(row == 1, base + i2, 0))
    wts_ref[...] = jnp.where(row == 0, p_group * w1, jnp.where(row == 1, p_group * (1.0 - w1), 0.0))


def _merge(x, p3, scans, y_c, mod, nmix, nffn, wg, wb, wo, hnw, sd, snw, e64, e128, wr, br):
    bsz, t_len, _ = x.shape
    tm = min(256, t_len)
    nt = t_len // tm
    row = lambda w: pl.BlockSpec((None, tm, w), lambda b, i: (b, i, 0))
    slab = lambda s: pl.BlockSpec((None, None, tm, SLAB), lambda b, i: (s, b, i, 0))
    full = lambda a: pl.BlockSpec(a.shape, lambda b, i: (0,) * a.ndim)
    consts = [nmix, nffn, wg, wb, wo, hnw, sd, snw, e64, e128, wr, br]
    lane_out = pl.BlockSpec((8, tm), lambda b, i: (0, b * nt + i))
    return pl.pallas_call(
        _merge_body,
        grid=(bsz, nt),
        in_specs=([row(D_MODEL)] + [slab(s) for s in (S_YA, S_G, S_Z, S_XS)] + [row(SLAB)] * 5
                  + [pl.BlockSpec((None, 8, D_MODEL), lambda b, i: (b, 0, 0))] + [full(a) for a in consts]),
        out_specs=[row(D_MODEL), row(D_MODEL), lane_out, lane_out],
        out_shape=[jax.ShapeDtypeStruct(x.shape, F32), jax.ShapeDtypeStruct(x.shape, F32),
                   jax.ShapeDtypeStruct((8, bsz * t_len), I32), jax.ShapeDtypeStruct((8, bsz * t_len), F32)],
        compiler_params=_params("parallel", "parallel"),
        name="merge_router",
    )(x, p3, p3, p3, p3, *scans, y_c, mod, *consts)


def _rank_body(ids_ref, tri_ref, rank_ref, cnt_ref, carry_ref, *, nsteps):
    i = pl.program_id(0)

    @pl.when(i == 0)
    def _():
        carry_ref[...] = jnp.zeros_like(carry_ref)

    tr = ids_ref.shape[1]
    eidx = lax.broadcasted_iota(I32, (N_EXPERTS, tr), 0)
    oh0 = jnp.where(eidx == ids_ref[0:1, :], 1.0, 0.0)
    oh1 = jnp.where(eidx == ids_ref[1:2, :], 1.0, 0.0)
    both = oh0 + oh1
    before = jnp.dot(both.astype(BF16), tri_ref[...], preferred_element_type=F32)
    pos = before + carry_ref[...]
    r0 = jnp.sum(oh0 * pos, axis=0, keepdims=True)
    r1 = jnp.sum(oh1 * pos, axis=0, keepdims=True)
    row = lax.broadcasted_iota(I32, (8, tr), 0)
    rank_ref[...] = jnp.where(row == 0, r0, jnp.where(row == 1, r1, 0.0)).astype(I32)
    carry_ref[...] = carry_ref[...] + jnp.sum(both, axis=1, keepdims=True)

    @pl.when(i == nsteps - 1)
    def _():
        cnt_ref[...] = carry_ref[...].astype(I32)


def _moe_rank(ids):
    n = ids.shape[1]
    tr = _tile(n, 1024)
    tri = jnp.triu(jnp.ones((tr, tr), BF16), k=1)
    return pl.pallas_call(
        functools.partial(_rank_body, nsteps=n // tr),
        grid=(n // tr,),
        in_specs=[pl.BlockSpec((8, tr), lambda i: (0, i)), pl.BlockSpec((tr, tr), lambda i: (0, 0))],
        out_specs=[pl.BlockSpec((8, tr), lambda i: (0, i)), pl.BlockSpec((N_EXPERTS, 1), lambda i: (0, 0))],
        out_shape=[jax.ShapeDtypeStruct((8, n), I32), jax.ShapeDtypeStruct((N_EXPERTS, 1), I32)],
        scratch_shapes=[pltpu.VMEM((N_EXPERTS, 1), F32)],
        compiler_params=_params("arbitrary"),
        name="moe_rank",
    )(ids, tri)


def _row_copy(src_hbm, dst_hbm, sem, src_row, dst_row):
    return pltpu.make_async_copy(src_hbm.at[pl.ds(src_row, 1)], dst_hbm.at[pl.ds(dst_row, 1)], sem)


def _dispatch_body(dest_ref, f_hbm, buf_in, buf_hbm, sem, *, tg):
    del buf_in
    base = pl.program_id(0) * tg

    def start(r, c):
        _row_copy(f_hbm, buf_hbm, sem, base + r, dest_ref[0, r]).start()
        _row_copy(f_hbm, buf_hbm, sem, base + r, dest_ref[1, r]).start()
        return c

    lax.fori_loop(0, tg, start, 0)

    def wait(r, c):
        _row_copy(f_hbm, buf_hbm, sem, 0, 0).wait()
        _row_copy(f_hbm, buf_hbm, sem, 0, 0).wait()
        return c

    lax.fori_loop(0, tg, wait, 0)


def _moe_dispatch(f2d, dest, buf):
    n = f2d.shape[0]
    tg = _tile(n, 512)
    return pl.pallas_call(
        functools.partial(_dispatch_body, tg=tg),
        grid=(n // tg,),
        in_specs=[pl.BlockSpec((2, tg), lambda i: (0, i), memory_space=pltpu.SMEM),
                  pl.BlockSpec(memory_space=pl.ANY), pl.BlockSpec(memory_space=pl.ANY)],
        out_specs=pl.BlockSpec(memory_space=pl.ANY),
        out_shape=jax.ShapeDtypeStruct(buf.shape, buf.dtype),
        scratch_shapes=[pltpu.SemaphoreType.DMA],
        input_output_aliases={2: 0},
        compiler_params=_params("arbitrary"),
        name="moe_dispatch",
    )(dest, f2d, buf)


def _ffn_body(be_ref, nu_ref, x_ref, w1_ref, w3_ref, w2_ref, o_ref):
    del be_ref

    @pl.when(pl.program_id(0) < nu_ref[0])
    def _():
        xb = x_ref[...].astype(BF16)
        a = jnp.dot(xb, w1_ref[...].astype(BF16), preferred_element_type=F32)
        b = jnp.dot(xb, w3_ref[...].astype(BF16), preferred_element_type=F32)
        hid = (a * _sigmoid(a) * b).astype(BF16)
        o_ref[...] = jnp.dot(hid, w2_ref[...].astype(BF16), preferred_element_type=F32)

    @pl.when(pl.program_id(0) >= nu_ref[0])
    def _():
        o_ref[...] = jnp.zeros_like(o_ref)


def _moe_ffn(buf, block_expert, n_used, w1, w3, w2):
    nblk = buf.shape[0] // MOE_ROWS
    wspec = lambda shape: pl.BlockSpec((None,) + shape, lambda i, be, nu: (be[i], 0, 0))
    return pl.pallas_call(
        _ffn_body,
        grid_spec=pltpu.PrefetchScalarGridSpec(
            num_scalar_prefetch=2,
            grid=(nblk,),
            in_specs=[pl.BlockSpec((MOE_ROWS, D_MODEL), lambda i, be, nu: (i, 0)),
                      wspec((D_MODEL, D_FF)), wspec((D_MODEL, D_FF)), wspec((D_FF, D_MODEL))],
            out_specs=pl.BlockSpec((MOE_ROWS, D_MODEL), lambda i, be, nu: (i, 0))),
        out_shape=jax.ShapeDtypeStruct(buf.shape, F32),
        compiler_params=_params("arbitrary"),
        name="moe_ffn",
    )(block_expert, n_used, buf, w1, w3, w2)


def _combine_body(dest_ref, x_ref, wts_ref, mod_ref, fnw_ref, y_hbm, o_ref, rows_ref, sem, *, tc, final):
    def start(r, c):
        pltpu.make_async_copy(y_hbm.at[pl.ds(dest_ref[0, r], 1)], rows_ref.at[0, pl.ds(r, 1)], sem).start()
        pltpu.make_async_copy(y_hbm.at[pl.ds(dest_ref[1, r], 1)], rows_ref.at[1, pl.ds(r, 1)], sem).start()
        return c

    lax.fori_loop(0, tc, start, 0)

    def wait(r, c):
        pltpu.make_async_copy(y_hbm.at[pl.ds(0, 1)], rows_ref.at[0, pl.ds(0, 1)], sem).wait()
        pltpu.make_async_copy(y_hbm.at[pl.ds(0, 1)], rows_ref.at[1, pl.ds(0, 1)], sem).wait()
        return c

    lax.fori_loop(0, tc, wait, 0)
    eye = jnp.where(lax.broadcasted_iota(I32, (tc, tc), 0) == lax.broadcasted_iota(I32, (tc, tc), 1), 1.0, 0.0)
    wcol = _nt(eye, wts_ref[...], precision=HIGHEST)
    y = wcol[:, 0:1] * rows_ref[0] + wcol[:, 1:2] * rows_ref[1]
    out = x_ref[...] + mod_ref[5:6, :] * y
    if final:
        ms = jnp.mean(out * out, axis=-1, keepdims=True)
        out = out * lax.rsqrt(ms + EPS) * fnw_ref[...]
    o_ref[...] = out


def _moe_combine(x, dest, wts, mod, fnw, y_sorted, col0, final):
    bsz, t_len, _ = x.shape
    tc = min(256, t_len)
    nt = t_len // tc
    c0 = col0 // tc
    return pl.pallas_call(
        functools.partial(_combine_body, tc=tc, final=final),
        grid=(bsz, nt),
        in_specs=[pl.BlockSpec((2, tc), lambda b, i: (0, c0 + b * nt + i), memory_space=pltpu.SMEM),
                  pl.BlockSpec((None, tc, D_MODEL), lambda b, i: (b, i, 0)),
                  pl.BlockSpec((8, tc), lambda b, i: (0, c0 + b * nt + i)),
                  pl.BlockSpec((None, 8, D_MODEL), lambda b, i: (b, 0, 0)),
                  pl.BlockSpec((1, D_MODEL), lambda b, i: (0, 0)),
                  pl.BlockSpec(memory_space=pl.ANY)],
        out_specs=pl.BlockSpec((None, tc, D_MODEL), lambda b, i: (b, i, 0)),
        out_shape=jax.ShapeDtypeStruct(x.shape, F32),
        scratch_shapes=[pltpu.VMEM((2, tc, D_MODEL), F32), pltpu.SemaphoreType.DMA],
        compiler_params=_params("arbitrary", "arbitrary"),
        name="moe_combine",
    )(dest, x, wts, mod, fnw, y_sorted)


def _moe(f_list, ids, wts, w1, w3, w2):
    n = ids.shape[1]
    rank, counts = _moe_rank(ids)
    counts = counts[:, 0]
    padded = (counts + MOE_ROWS - 1) // MOE_ROWS * MOE_ROWS
    pad_ends = jnp.cumsum(padded)
    pad_starts = pad_ends - padded
    dest = pad_starts[ids[:2]] + rank[:2]
    nblk = (2 * n) // MOE_ROWS + N_EXPERTS
    block_start = jnp.arange(nblk, dtype=I32) * MOE_ROWS
    block_expert = jnp.minimum(jnp.sum(block_start[:, None] >= pad_ends[None, :], axis=-1), N_EXPERTS - 1)
    n_used = (pad_ends[-1:] // MOE_ROWS).astype(I32)
    buf = jnp.zeros((nblk * MOE_ROWS, D_MODEL), F32)
    col = 0
    for f in f_list:
        f2d = f.reshape(-1, D_MODEL)
        buf = _moe_dispatch(f2d, lax.slice_in_dim(dest, col, col + f2d.shape[0], axis=1), buf)
        col += f2d.shape[0]
    y_sorted = _moe_ffn(buf, block_expert.astype(I32), n_used, w1, w3, w2)
    return y_sorted, dest


def _expand_heads(a, groups):
    idx = np.concatenate([np.arange(HEAD) + (h * groups // N_HEADS) * HEAD for h in range(N_HEADS)])
    return a[..., idx]


def kernel(x, c, ctx, c_ctx, ada_w, ada_b, norm_mix_w, norm_ffn_w, w_in, conv_a_w, conv_a_b, hgrn_lb,
           hgrn_norm_w, ssm_conv_w, ssm_conv_b, ssm_A_log, ssm_dt_bias, ssm_D, ssm_norm_w, w_branch, w_out,
           router_group_w, router_group_b, router_expert_w, router_expert_b, moe_w1, moe_w3, moe_w2,
           final_norm_w):
    depth = ada_w.shape[0]
    bsz, t_len, _ = x.shape
    t_ctx = ctx.shape[1]
    n_lat = bsz * t_len

    cond = jnp.concatenate([c, c_ctx[None, :], jnp.zeros((8 - bsz - 1, D_MODEL), F32)], axis=0)
    mods = _ada(cond, ada_w, ada_b)
    sc = {k: jnp.asarray(v) for k, v in _scan_constants().items()}
    fft_l = tuple(jnp.asarray(a) for a in _fft_constants(t_len))
    fft_c = tuple(jnp.asarray(a) for a in _fft_constants(t_ctx))
    e64 = sc["ebd"].astype(BF16)
    lane128 = np.arange(SLAB) // (2 * HEAD)
    e128 = jnp.asarray(lane128[:, None] == lane128[None, :], BF16)
    zero_state = jnp.zeros((bsz, 4, SLAB, SLAB), F32)
    fnw = final_norm_w.reshape(1, D_MODEL)

    for l in range(depth):
        last = l == depth - 1
        six = mods[l].reshape(8, 6, D_MODEL)
        mod_l = jnp.pad(six[:bsz], ((0, 0), (0, 2), (0, 0)))
        mod_c = jnp.broadcast_to(jnp.pad(six[bsz], ((0, 2), (0, 0))), (bsz, 8, D_MODEL))

        wl = w_in[l]
        xbc0 = 10 * SLAB
        w_b = _expand_heads(wl[:, xbc0 + SLAB:xbc0 + SLAB + 2 * HEAD], 2)
        w_c = _expand_heads(wl[:, xbc0 + SLAB + 2 * HEAD:xbc0 + 2 * SLAB], 2)
        dt0 = 12 * SLAB
        w_dt = jnp.repeat(wl[:, dt0:dt0 + 2 * N_HEADS], HEAD, axis=1)
        w_ext = jnp.concatenate([wl[:, :xbc0 + SLAB], w_b, w_c, w_dt], axis=1).astype(BF16)
        w_gate = wl[:, dt0 + 2 * N_HEADS:].astype(BF16)
        cw = ssm_conv_w[l]
        scw = jnp.concatenate([cw[:, :SLAB], _expand_heads(cw[:, SLAB:SLAB + 2 * HEAD], 2),
                               _expand_heads(cw[:, SLAB + 2 * HEAD:], 2)], axis=1)
        cbias = ssm_conv_b[l][None, :]
        scb = jnp.concatenate([cbias[:, :SLAB], _expand_heads(cbias[:, SLAB:SLAB + 2 * HEAD], 2),
                               _expand_heads(cbias[:, SLAB + 2 * HEAD:], 2)], axis=1)
        caw, cab = conv_a_w[l], conv_a_b[l][None, :]
        nmix, nffn = norm_mix_w[l][None, :], norm_ffn_w[l][None, :]
        alog_lane = jnp.repeat(ssm_A_log[l], HEAD, axis=1)
        dtbias_lane = jnp.repeat(ssm_dt_bias[l], HEAD, axis=1)
        hnw = jnp.tile(hgrn_norm_w[l], N_HEADS)[None, :]
        sd = jnp.repeat(ssm_D[l], HEAD)[None, :]
        snw = ssm_norm_w[l][None, :]
        wb, wo = w_branch[l].astype(BF16), w_out[l].astype(BF16)
        wr = jnp.concatenate([router_group_w[l].T, jnp.zeros((8 - N_GROUPS, D_MODEL), F32),
                              router_expert_w[l].T], axis=0)
        br = jnp.concatenate([router_group_b[l], jnp.zeros((8 - N_GROUPS,), F32),
                              router_expert_b[l]])[:, None]
        merge_w = (nmix, nffn, w_gate, wb, wo, hnw, sd, snw, e64, e128, wr, br)

        p3_c = _inproj(ctx, mod_c, nmix, w_ext, caw, cab, scw, scb)
        *scans_c, states = _scan(p3_c, hgrn_lb, alog_lane, dtbias_lane, zero_state, sc, l)
        p3_l = _inproj(x, mod_l, nmix, w_ext, caw, cab, scw, scb)
        *scans_l, _ = _scan(p3_l, hgrn_lb, alog_lane, dtbias_lane, states, sc, l)
        yc_l = _fourier(p3_l, fft_l)
        x, f_l, ids, wts = _merge(x, p3_l, scans_l, yc_l, mod_l, *merge_w)
        f_list = [f_l]
        if not last:
            yc_c = _fourier(p3_c, fft_c)
            ctx, f_c, ids_c, wts_c = _merge(ctx, p3_c, scans_c, yc_c, mod_c, *merge_w)
            f_list.append(f_c)
            ids = jnp.concatenate([ids, ids_c], axis=1)
            wts = jnp.concatenate([wts, wts_c], axis=1)
        y_sorted, dest = _moe(f_list, ids, wts, moe_w1[l], moe_w3[l], moe_w2[l])
        x = _moe_combine(x, dest, wts, mod_l, fnw, y_sorted, 0, last)
        if not last:
            ctx = _moe_combine(ctx, dest, wts, mod_c, fnw, y_sorted, n_lat, False)
    return x
```

```python
import functools

import numpy as np
import jax
import jax.numpy as jnp
from jax import lax
from jax.experimental import pallas as pl
from jax.experimental.pallas import tpu as pltpu

F32, BF16, I32 = jnp.float32, jnp.bfloat16, jnp.int32
HIGHEST = lax.Precision.HIGHEST
EPS = 1e-6

D_MODEL = 1024
SLAB = 256
N_HEADS = 4
HEAD = 64
C_GDIM = 64
N_BRANCH = 4
CHUNK = 64
N_LEVELS = 6
N_GROUPS, EXPERTS_PER_GROUP = 4, 8
N_EXPERTS = N_GROUPS * EXPERTS_PER_GROUP
D_FF = 512
ROUTER_ROWS = 8 + N_EXPERTS
MOE_ROWS = 256
VMEM_LIMIT = 56 * 1024 * 1024

S_YA, S_Q, S_FF, S_FB, S_I, S_G, S_FOUR, S_Z, S_XS, S_BX, S_CX, S_DTF, S_DTB = range(13)
N_SLABS = 13


def _sigmoid(x):
    return 1.0 / (1.0 + jnp.exp(-x))


def _softplus(x):
    return jnp.maximum(x, 0.0) + jnp.log1p(jnp.exp(-jnp.abs(x)))


def _tile(n, pref):
    t = pref
    while n % t:
        t //= 2
    return t


def _params(*sem):
    return pltpu.CompilerParams(dimension_semantics=sem, vmem_limit_bytes=VMEM_LIMIT)


def _nt(a, b, **kw):
    return lax.dot_general(a, b, (((1,), (1,)), ((), ())), preferred_element_type=F32, **kw)


def _tn(a, b, **kw):
    return lax.dot_general(a, b, (((0,), (0,)), ((), ())), preferred_element_type=F32, **kw)


def _ada_body(s_ref, w_ref, b_ref, o_ref):
    s = s_ref[...]
    s = s * _sigmoid(s)
    o_ref[...] = jnp.dot(s, w_ref[...], precision=HIGHEST, preferred_element_type=F32) + b_ref[...]


def _ada(cond, ada_w, ada_b):
    depth = ada_w.shape[0]
    n6 = ada_w.shape[2] // D_MODEL
    return pl.pallas_call(
        _ada_body,
        grid=(depth, n6),
        in_specs=[pl.BlockSpec((8, D_MODEL), lambda l, j: (0, 0)),
                  pl.BlockSpec((None, D_MODEL, D_MODEL), lambda l, j: (l, 0, j)),
                  pl.BlockSpec((None, 1, D_MODEL), lambda l, j: (l, 0, j))],
        out_specs=pl.BlockSpec((None, 8, D_MODEL), lambda l, j: (l, 0, j)),
        out_shape=jax.ShapeDtypeStruct((depth, 8, n6 * D_MODEL), F32),
        compiler_params=_params("parallel", "parallel"),
        name="ada_mod",
    )(cond, ada_w, ada_b.reshape(depth, 1, -1))


def _norm_mod(x, nw, shift, scale):
    ms = jnp.mean(x * x, axis=-1, keepdims=True)
    return (x * lax.rsqrt(ms + EPS) * nw) * (1.0 + scale) + shift


def _inproj_body(x_ref, xp_ref, xn_ref, mod_ref, nw_ref, w_ref, caw_ref, cab_ref, scw_ref, scb_ref,
                 p_ref, *, tm, nt):
    i = pl.program_id(1)
    xe = jnp.concatenate([xp_ref[...], x_ref[...], xn_ref[...]], axis=0)
    h = _norm_mod(xe, nw_ref[...], mod_ref[0:1, :], mod_ref[1:2, :])
    pr = jnp.dot(h.astype(BF16), w_ref[...], preferred_element_type=F32)
    rows = lax.broadcasted_iota(I32, (tm + 16, 1), 0)
    lo = jnp.where(i > 0, 0, 8)
    hi = jnp.where(i < nt - 1, tm + 16, tm + 8)
    vm = jnp.where((rows >= lo) & (rows < hi), 1.0, 0.0)

    def conv3(u, w, b):
        u = u * vm
        return u[7:tm + 7] * w[0:1, :] + u[8:tm + 8] * w[1:2, :] + u[9:tm + 9] * w[2:3, :] + b[...]

    ca = conv3(pr[:, SLAB:2 * SLAB] * pr[:, 2 * SLAB:3 * SLAB], caw_ref, cab_ref)
    p_ref[S_YA] = pr[8:tm + 8, 0:SLAB] * ca
    for s in range(7):
        p_ref[S_Q + s] = pr[8:tm + 8, (3 + s) * SLAB:(4 + s) * SLAB]
    cs = conv3(pr[:, 10 * SLAB:13 * SLAB], scw_ref, scb_ref)
    cs = cs * _sigmoid(cs)
    for s in range(3):
        p_ref[S_XS + s] = cs[:, s * SLAB:(s + 1) * SLAB]
    p_ref[S_DTF] = pr[8:tm + 8, 13 * SLAB:14 * SLAB]
    p_ref[S_DTB] = pr[8:tm + 8, 14 * SLAB:15 * SLAB]


def _inproj(x, mod, nw, w_ext, caw, cab, scw, scb):
    bsz, t_len, _ = x.shape
    tm = min(512, t_len)
    nt = t_len // tm
    t8 = tm // 8
    full = lambda shape: pl.BlockSpec(shape, lambda b, i: (0,) * len(shape))
    return pl.pallas_call(
        functools.partial(_inproj_body, tm=tm, nt=nt),
        grid=(bsz, nt),
        in_specs=[pl.BlockSpec((None, tm, D_MODEL), lambda b, i: (b, i, 0)),
                  pl.BlockSpec((None, 8, D_MODEL), lambda b, i: (b, jnp.maximum(i * t8 - 1, 0), 0)),
                  pl.BlockSpec((None, 8, D_MODEL), lambda b, i: (b, jnp.minimum((i + 1) * t8, t_len // 8 - 1), 0)),
                  pl.BlockSpec((None, 8, D_MODEL), lambda b, i: (b, 0, 0)),
                  full((1, D_MODEL)), full(w_ext.shape), full(caw.shape), full(cab.shape),
                  full(scw.shape), full(scb.shape)],
        out_specs=pl.BlockSpec((N_SLABS, None, tm, SLAB), lambda b, i: (0, b, i, 0)),
        out_shape=jax.ShapeDtypeStruct((N_SLABS, bsz, t_len, SLAB), F32),
        compiler_params=_params("parallel", "parallel"),
        name="in_proj",
    )(x, x, x, mod, nw, w_ext, caw, cab, scw, scb)


def _scan_constants():
    t = np.arange(CHUNK)[:, None]
    r = np.arange(CHUNK)[None, :]
    out = {}
    for name, fwd in (("f", True), ("b", False)):
        blocks = [(r <= t) if fwd else (r >= t), (r > t) if fwd else (r < t)]
        qs, ks, masks = [], [], []
        for lvl in range(N_LEVELS):
            m = 1 << lvl
            blk = t // (2 * m)
            ref = blk * 2 * m + m - 1
            upper = (t % (2 * m)) >= m
            s_blk = (r // (2 * m))
            s_upper = (r % (2 * m)) >= m
            if fwd:
                qs.append(upper & (r > ref) & (r <= t))
                ks.append((~upper) & (r > t) & (r <= ref))
                masks.append(upper & (~s_upper) & (s_blk == blk))
            else:
                qs.append((~upper) & (r >= t) & (r <= ref))
                ks.append(upper & (r > ref) & (r < t))
                masks.append((~upper) & s_upper & (s_blk == blk))
        out["cm_" + name] = np.concatenate(blocks + qs + ks, axis=0).astype(np.float32)
        out["lm_" + name] = np.stack([np.tile(mk, (N_HEADS, 1)) for mk in masks]).astype(np.float32)
    lane_head = np.arange(SLAB) // HEAD
    out["hm"] = (lane_head[None, :] == np.arange(8)[:, None]).astype(np.float32)
    out["ebd"] = (lane_head[:, None] == lane_head[None, :]).astype(np.float32)
    return out


def _lin_attn_chunk(qv, kv, vv, logf, s_ref, k, cm, lm_ref, hm_ref, ebd, fwd):
    hi = logf.astype(BF16)
    lo = (logf - hi.astype(F32)).astype(BF16)
    ex = jnp.dot(cm, jnp.concatenate([hi, lo], axis=1), preferred_element_type=F32)
    e = jnp.exp(ex[:, :SLAB] + ex[:, SLAB:])
    c = CHUNK
    p = jnp.zeros((N_HEADS * c, c), F32)
    for lvl in range(N_LEVELS):
        qh = qv * e[(2 + lvl) * c:(3 + lvl) * c]
        kh = (kv * e[(2 + N_LEVELS + lvl) * c:(3 + N_LEVELS + lvl) * c]).astype(BF16)
        qs = jnp.concatenate([qh * hm_ref[h:h + 1, :] for h in range(N_HEADS)], axis=0).astype(BF16)
        p = p + _nt(qs, kh) * lm_ref[lvl]
    state = s_ref[k]
    vb = vv.astype(BF16)
    o = jnp.dot((qv * kv).astype(BF16), ebd, preferred_element_type=F32) * vv
    rv = jnp.dot(p.astype(BF16), vb, preferred_element_type=F32)
    for h in range(N_HEADS):
        o = o + rv[h * c:(h + 1) * c] * hm_ref[h:h + 1, :]
    o = o + _nt((qv * e[0:c]).astype(BF16), state.astype(BF16))
    upd = _tn(vb, (kv * e[c:2 * c]).astype(BF16))
    tot = e[c - 1:c] if fwd else e[0:1]
    s_ref[k] = state * tot + upd * ebd.astype(F32)
    return o


def _scan_body(qf, ff, vf, xsf, bxf, cxf, dtf, qb, fb, vb, xsb, bxb, cxb, dtb,
               lb_ref, alog_ref, dtbias_ref, cmf_ref, cmb_ref, lmf_ref, lmb_ref, hm_ref, ebd_ref, s0_ref,
               ohf, ohb, osf, osb, sout_ref, s_ref, *, layer, nck, nsteps):
    i = pl.program_id(1)

    @pl.when(i == 0)
    def _():
        s_ref[...] = s0_ref[...]

    ebd = ebd_ref[...]
    cmf, cmb = cmf_ref[...], cmb_ref[...]

    def lower_bound(d):
        rows = lb_ref[d]
        ex = jnp.exp(rows - jnp.max(rows, axis=0, keepdims=True))
        prob = ex / jnp.sum(ex, axis=0, keepdims=True)
        lb = jnp.zeros((1, SLAB), F32)
        for j in range(1, layer + 1):
            lb = lb + prob[j:j + 1]
        return lb

    lbs = [lower_bound(0), lower_bound(1)]

    def hgrn_inputs(q_raw, f_raw, lb):
        q = q_raw * _sigmoid(q_raw)
        log_sig = jnp.minimum(f_raw, 0.0) - jnp.log1p(jnp.exp(-jnp.abs(f_raw)))
        a = jnp.log(lb)
        b = jnp.log1p(-lb) + log_sig
        logf = jnp.maximum(a, b) + jnp.log1p(jnp.exp(-jnp.abs(a - b)))
        kk = (1.0 - lb) * _sigmoid(-f_raw)
        return q, kk, logf

    def ssd_inputs(bx, dt_raw, d):
        dt = _softplus(dt_raw + dtbias_ref[d:d + 1, :])
        return bx * dt, -jnp.exp(alog_ref[d:d + 1, :]) * dt

    def body(j, carry):
        rf = pl.ds(pl.multiple_of(j * CHUNK, CHUNK), CHUNK)
        rb = pl.ds(pl.multiple_of((nck - 1 - j) * CHUNK, CHUNK), CHUNK)
        q, kk, logf = hgrn_inputs(qf[rf, :], ff[rf, :], lbs[0])
        ohf[rf, :] = _lin_attn_chunk(q, kk, vf[rf, :], logf, s_ref, 0, cmf, lmf_ref, hm_ref, ebd, True)
        q, kk, logf = hgrn_inputs(qb[rb, :], fb[rb, :], lbs[1])
        ohb[rb, :] = _lin_attn_chunk(q, kk, vb[rb, :], logf, s_ref, 1, cmb, lmb_ref, hm_ref, ebd, False)
        kk, logf = ssd_inputs(bxf[rf, :], dtf[rf, :], 0)
        osf[rf, :] = _lin_attn_chunk(cxf[rf, :], kk, xsf[rf, :], logf, s_ref, 2, cmf, lmf_ref, hm_ref, ebd, True)
        kk, logf = ssd_inputs(bxb[rb, :], dtb[rb, :], 1)
        osb[rb, :] = _lin_attn_chunk(cxb[rb, :], kk, xsb[rb, :], logf, s_ref, 3, cmb, lmb_ref, hm_ref, ebd, False)
        return carry

    lax.fori_loop(0, nck, body, 0)

    @pl.when(i == nsteps - 1)
    def _():
        sout_ref[...] = s_ref[...]


def _scan(p3, hgrn_lb, alog_lane, dtbias_lane, s0, consts, layer):
    _, bsz, t_len, _ = p3.shape
    cb = min(256, t_len)
    nb = t_len // cb
    slab_f = lambda s: pl.BlockSpec((None, None, cb, SLAB), lambda b, i: (s, b, i, 0))
    slab_b = lambda s: pl.BlockSpec((None, None, cb, SLAB), lambda b, i: (s, b, nb - 1 - i, 0))
    full = lambda a: pl.BlockSpec(a.shape, lambda b, i: (0,) * a.ndim)
    cm_f, cm_b = consts["cm_f"].astype(BF16), consts["cm_b"].astype(BF16)
    small = [hgrn_lb, alog_lane, dtbias_lane, cm_f, cm_b, consts["lm_f"], consts["lm_b"], consts["hm"],
             consts["ebd"].astype(BF16)]
    o_f = pl.BlockSpec((None, cb, SLAB), lambda b, i: (b, i, 0))
    o_b = pl.BlockSpec((None, cb, SLAB), lambda b, i: (b, nb - 1 - i, 0))
    st = pl.BlockSpec((None, 4, SLAB, SLAB), lambda b, i: (b, 0, 0, 0))
    o_shape = jax.ShapeDtypeStruct((bsz, t_len, SLAB), F32)
    return pl.pallas_call(
        functools.partial(_scan_body, layer=layer, nck=cb // CHUNK, nsteps=nb),
        grid=(bsz, nb),
        in_specs=([slab_f(s) for s in (S_Q, S_FF, S_I, S_XS, S_BX, S_CX, S_DTF)]
                  + [slab_b(s) for s in (S_Q, S_FB, S_I, S_XS, S_BX, S_CX, S_DTB)]
                  + [full(a) for a in small] + [st]),
        out_specs=[o_f, o_b, o_f, o_b, st],
        out_shape=[o_shape, o_shape, o_shape, o_shape, jax.ShapeDtypeStruct(s0.shape, F32)],
        scratch_shapes=[pltpu.VMEM((4, SLAB, SLAB), F32)],
        compiler_params=_params("parallel", "arbitrary"),
        name="chunk_scan",
    )(*([p3] * 14), *small, s0)


def _fft_sizes(t_len):
    tf = 64 if t_len >= 4096 else 16
    return tf, t_len // tf


def _fft_constants(t_len):
    tf, ts = _fft_sizes(t_len)
    c = np.arange(SLAB)
    same = (c[:, None] // C_GDIM) == (c[None, :] // C_GDIM)
    ang = 2 * np.pi * ((c[:, None] % C_GDIM) * (c[None, :] % C_GDIM) % C_GDIM) / C_GDIM
    chan = np.concatenate([np.cos(ang) * same, -np.sin(ang) * same], axis=1)
    k2 = np.arange(ts)
    a = 2 * np.pi * (k2[:, None] * k2[None, :] % ts) / ts
    flat = np.block([[np.cos(a), np.sin(a)], [-np.sin(a), np.cos(a)]])
    k1 = np.arange(tf)[None, :, None]
    tfi = np.arange(tf)[None, None, :]
    kk2 = np.arange(ts)[:, None, None]
    th = 2 * np.pi * (((tfi * k1 * ts) + tfi * kk2) % t_len) / t_len
    scale = 1.0 / np.sqrt(t_len * C_GDIM)
    slab = np.concatenate([np.cos(th), np.sin(th)], axis=2) * scale
    return chan.astype(np.float32), flat.astype(np.float32), slab.astype(np.float32)


def _fft_flat_body(x_ref, chan_ref, flat_ref, y_ref, *, ts, groups):
    for g in range(groups):
        xg = x_ref[:, g * SLAB:(g + 1) * SLAB]
        a = jnp.dot(xg, chan_ref[...], precision=HIGHEST, preferred_element_type=F32)
        z = jnp.concatenate([a[:, :SLAB], a[:, SLAB:]], axis=0)
        y = jnp.dot(flat_ref[...], z, precision=HIGHEST, preferred_element_type=F32)
        y_ref[0, :, g * SLAB:(g + 1) * SLAB] = y[:ts]
        y_ref[1, :, g * SLAB:(g + 1) * SLAB] = y[ts:]


def _fft_slab_body(y_ref, m_ref, o_ref, *, kb):
    for j in range(kb):
        z = jnp.concatenate([y_ref[0, j], y_ref[1, j]], axis=0)
        o_ref[j] = jnp.dot(m_ref[j], z, precision=HIGHEST, preferred_element_type=F32)


def _fourier(p3, consts):
    _, bsz, t_len, _ = p3.shape
    tf, ts = _fft_sizes(t_len)
    chan, flat, slab = consts
    groups = min(8, tf)
    lanes = groups * SLAB
    y = pl.pallas_call(
        functools.partial(_fft_flat_body, ts=ts, groups=groups),
        grid=(bsz, tf // groups),
        in_specs=[pl.BlockSpec((None, ts, lanes), lambda b, j: (b, 0, j)),
                  pl.BlockSpec(chan.shape, lambda b, j: (0, 0)),
                  pl.BlockSpec(flat.shape, lambda b, j: (0, 0))],
        out_specs=pl.BlockSpec((None, 2, ts, lanes), lambda b, j: (b, 0, 0, j)),
        out_shape=jax.ShapeDtypeStruct((bsz, 2, ts, tf * SLAB), F32),
        compiler_params=_params("parallel", "parallel"),
        name="fft_flat",
    )(p3[S_FOUR].reshape(bsz, ts, tf * SLAB), chan, flat)
    kb = min(8, ts)
    out = pl.pallas_call(
        functools.partial(_fft_slab_body, kb=kb),
        grid=(bsz, ts // kb),
        in_specs=[pl.BlockSpec((None, 2, kb, tf, SLAB), lambda b, j: (b, 0, j, 0, 0)),
                  pl.BlockSpec((kb, tf, 2 * tf), lambda b, j: (j, 0, 0))],
        out_specs=pl.BlockSpec((None, kb, tf, SLAB), lambda b, j: (b, j, 0, 0)),
        out_shape=jax.ShapeDtypeStruct((bsz, ts, tf, SLAB), F32),
        compiler_params=_params("parallel", "parallel"),
        name="fft_slab",
    )(y.reshape(bsz, 2, ts, tf, SLAB), slab)
    return out.transpose(0, 2, 1, 3).reshape(bsz, t_len, SLAB)


def _merge_body(x_ref, ya_ref, g_ref, z_ref, xs_ref, hf_ref, hb_ref, sf_ref, sb_ref, yc_ref,
                mod_ref, nmix_ref, nffn_ref, wg_ref, wb_ref, wo_ref, hnw_ref, sd_ref, snw_ref,
                e64_ref, e128_ref, wr_ref, br_ref,
                xo_ref, f_ref, ids_ref, wts_ref):
    x = x_ref[...]
    h = _norm_mod(x, nmix_ref[...], mod_ref[0:1, :], mod_ref[1:2, :]).astype(BF16)

    def group_rms(v, e_ref, width):
        ms = jnp.dot((v * v).astype(BF16), e_ref[...], preferred_element_type=F32) * (1.0 / width)
        return v * lax.rsqrt(ms + EPS)

    g = g_ref[...]
    y_b = group_rms(hf_ref[...] + hb_ref[...], e64_ref, HEAD) * hnw_ref[...] * (g * _sigmoid(g))
    z = z_ref[...]
    y_d = (sf_ref[...] + sb_ref[...] + sd_ref[...] * xs_ref[...]) * (z * _sigmoid(z))
    y_d = group_rms(y_d, e128_ref, 2 * HEAD) * snw_ref[...]
    branches = (ya_ref[...], y_b, yc_ref[...], y_d)
    merged = None
    for kb in range(N_BRANCH):
        gate = _sigmoid(jnp.dot(h, wg_ref[:, kb * D_MODEL:(kb + 1) * D_MODEL], preferred_element_type=F32))
        term = gate * jnp.dot(branches[kb].astype(BF16), wb_ref[kb], preferred_element_type=F32)
        merged = term if merged is None else merged + term
    mix = jnp.dot(merged.astype(BF16), wo_ref[...], preferred_element_type=F32)
    xo = x + mod_ref[2:3, :] * mix
    xo_ref[...] = xo
    f = _norm_mod(xo, nffn_ref[...], mod_ref[3:4, :], mod_ref[4:5, :])
    f_ref[...] = f

    lg = _nt(wr_ref[...], f, precision=HIGHEST) + br_ref[...]
    gl = lg[0:N_GROUPS]
    gidx = lax.broadcasted_iota(I32, gl.shape, 0)
    gmax = jnp.max(gl, axis=0, keepdims=True)
    g_top = jnp.min(jnp.where(gl == gmax, gidx, N_GROUPS), axis=0, keepdims=True)
    p_group = 1.0 / jnp.sum(jnp.exp(gl - gmax), axis=0, keepdims=True)
    e_in = jnp.zeros((EXPERTS_PER_GROUP, gl.shape[1]), F32)
    for grp in range(N_GROUPS):
        e_in = jnp.where(g_top == grp, lg[8 + 8 * grp:16 + 8 * grp], e_in)
    eidx = lax.broadcasted_iota(I32, e_in.shape, 0)
    v1 = jnp.max(e_in, axis=0, keepdims=True)
    i1 = jnp.min(jnp.where(e_in == v1, eidx, EXPERTS_PER_GROUP), axis=0, keepdims=True)
    rest = jnp.where(eidx == i1, -jnp.inf, e_in)
    v2 = jnp.max(rest, axis=0, keepdims=True)
    i2 = jnp.min(jnp.where(rest == v2, eidx, EXPERTS_PER_GROUP), axis=0, keepdims=True)
    w1 = 1.0 / (1.0 + jnp.exp(v2 - v1))
    row = lax.broadcasted_iota(I32, (8, gl.shape[1]), 0)
    base = g_top * EXPERTS_PER_GROUP
    ids_ref[...] = jnp.where(row == 0, base + i1, jnp.where(row == 1, base + i2, 0))
    wts_ref[...] = jnp.where(row == 0, p_group * w1, jnp.where(row == 1, p_group * (1.0 - w1), 0.0))


def _merge(x, p3, scans, y_c, mod, nmix, nffn, wg, wb, wo, hnw, sd, snw, e64, e128, wr, br):
    bsz, t_len, _ = x.shape
    tm = min(256, t_len)
    nt = t_len // tm
    row = lambda w: pl.BlockSpec((None, tm, w), lambda b, i: (b, i, 0))
    slab = lambda s: pl.BlockSpec((None, None, tm, SLAB), lambda b, i: (s, b, i, 0))
    full = lambda a: pl.BlockSpec(a.shape, lambda b, i: (0,) * a.ndim)
    consts = [nmix, nffn, wg, wb, wo, hnw, sd, snw, e64, e128, wr, br]
    lane_out = pl.BlockSpec((8, tm), lambda b, i: (0, b * nt + i))
    return pl.pallas_call(
        _merge_body,
        grid=(bsz, nt),
        in_specs=([row(D_MODEL)] + [slab(s) for s in (S_YA, S_G, S_Z, S_XS)] + [row(SLAB)] * 5
                  + [pl.BlockSpec((None, 8, D_MODEL), lambda b, i: (b, 0, 0))] + [full(a) for a in consts]),
        out_specs=[row(D_MODEL), row(D_MODEL), lane_out, lane_out],
        out_shape=[jax.ShapeDtypeStruct(x.shape, F32), jax.ShapeDtypeStruct(x.shape, F32),
                   jax.ShapeDtypeStruct((8, bsz * t_len), I32), jax.ShapeDtypeStruct((8, bsz * t_len), F32)],
        compiler_params=_params("parallel", "parallel"),
        name="merge_router",
    )(x, p3, p3, p3, p3, *scans, y_c, mod, *consts)


def _rank_body(ids_ref, tri_ref, rank_ref, cnt_ref, carry_ref, *, nsteps):
    i = pl.program_id(0)

    @pl.when(i == 0)
    def _():
        carry_ref[...] = jnp.zeros_like(carry_ref)

    tr = ids_ref.shape[1]
    eidx = lax.broadcasted_iota(I32, (N_EXPERTS, tr), 0)
    oh0 = jnp.where(eidx == ids_ref[0:1, :], 1.0, 0.0)
    oh1 = jnp.where(eidx == ids_ref[1:2, :], 1.0, 0.0)
    both = oh0 + oh1
    before = jnp.dot(both.astype(BF16), tri_ref[...], preferred_element_type=F32)
    pos = before + carry_ref[...]
    r0 = jnp.sum(oh0 * pos, axis=0, keepdims=True)
    r1 = jnp.sum(oh1 * pos, axis=0, keepdims=True)
    row = lax.broadcasted_iota(I32, (8, tr), 0)
    rank_ref[...] = jnp.where(row == 0, r0, jnp.where(row == 1, r1, 0.0)).astype(I32)
    carry_ref[...] = carry_ref[...] + jnp.sum(both, axis=1, keepdims=True)

    @pl.when(i == nsteps - 1)
    def _():
        cnt_ref[...] = carry_ref[...].astype(I32)


def _moe_rank(ids):
    n = ids.shape[1]
    tr = _tile(n, 1024)
    tri = jnp.triu(jnp.ones((tr, tr), BF16), k=1)
    return pl.pallas_call(
        functools.partial(_rank_body, nsteps=n // tr),
        grid=(n // tr,),
        in_specs=[pl.BlockSpec((8, tr), lambda i: (0, i)), pl.BlockSpec((tr, tr), lambda i: (0, 0))],
        out_specs=[pl.BlockSpec((8, tr), lambda i: (0, i)), pl.BlockSpec((N_EXPERTS, 1), lambda i: (0, 0))],
        out_shape=[jax.ShapeDtypeStruct((8, n), I32), jax.ShapeDtypeStruct((N_EXPERTS, 1), I32)],
        scratch_shapes=[pltpu.VMEM((N_EXPERTS, 1), F32)],
        compiler_params=_params("arbitrary"),
        name="moe_rank",
    )(ids, tri)


def _row_copy(src_hbm, dst_hbm, sem, src_row, dst_row):
    return pltpu.make_async_copy(src_hbm.at[pl.ds(src_row, 1)], dst_hbm.at[pl.ds(dst_row, 1)], sem)


def _dispatch_body(dest_ref, f_ref, buf_in, buf_hbm, sem, *, tg):
    del buf_in

    def start(r, c):
        _row_copy(f_ref, buf_hbm, sem, r, dest_ref[0, r]).start()
        _row_copy(f_ref, buf_hbm, sem, r, dest_ref[1, r]).start()
        return c

    lax.fori_loop(0, tg, start, 0)

    def wait(r, c):
        _row_copy(f_ref, buf_hbm, sem, 0, 0).wait()
        _row_copy(f_ref, buf_hbm, sem, 0, 0).wait()
        return c

    lax.fori_loop(0, tg, wait, 0)


def _moe_dispatch(f2d, dest, buf):
    n = f2d.shape[0]
    tg = _tile(n, 512)
    return pl.pallas_call(
        functools.partial(_dispatch_body, tg=tg),
        grid=(n // tg,),
        in_specs=[pl.BlockSpec((2, tg), lambda i: (0, i), memory_space=pltpu.SMEM),
                  pl.BlockSpec((tg, D_MODEL), lambda i: (i, 0)), pl.BlockSpec(memory_space=pl.ANY)],
        out_specs=pl.BlockSpec(memory_space=pl.ANY),
        out_shape=jax.ShapeDtypeStruct(buf.shape, buf.dtype),
        scratch_shapes=[pltpu.SemaphoreType.DMA],
        input_output_aliases={2: 0},
        compiler_params=_params("arbitrary"),
        name="moe_dispatch",
    )(dest, f2d, buf)


def _ffn_body(be_ref, nu_ref, x_ref, w1_ref, w3_ref, w2_ref, o_ref):
    del be_ref

    @pl.when(pl.program_id(0) < nu_ref[0])
    def _():
        xb = x_ref[...].astype(BF16)
        a = jnp.dot(xb, w1_ref[...].astype(BF16), preferred_element_type=F32)
        b = jnp.dot(xb, w3_ref[...].astype(BF16), preferred_element_type=F32)
        hid = (a * _sigmoid(a) * b).astype(BF16)
        o_ref[...] = jnp.dot(hid, w2_ref[...].astype(BF16), preferred_element_type=F32)

    @pl.when(pl.program_id(0) >= nu_ref[0])
    def _():
        o_ref[...] = jnp.zeros_like(o_ref)


def _moe_ffn(buf, block_expert, n_used, w1, w3, w2):
    nblk = buf.shape[0] // MOE_ROWS
    wspec = lambda shape: pl.BlockSpec((None,) + shape, lambda i, be, nu: (be[i], 0, 0))
    return pl.pallas_call(
        _ffn_body,
        grid_spec=pltpu.PrefetchScalarGridSpec(
            num_scalar_prefetch=2,
            grid=(nblk,),
            in_specs=[pl.BlockSpec((MOE_ROWS, D_MODEL), lambda i, be, nu: (i, 0)),
                      wspec((D_MODEL, D_FF)), wspec((D_MODEL, D_FF)), wspec((D_FF, D_MODEL))],
            out_specs=pl.BlockSpec((MOE_ROWS, D_MODEL), lambda i, be, nu: (i, 0))),
        out_shape=jax.ShapeDtypeStruct(buf.shape, F32),
        compiler_params=_params("arbitrary"),
        name="moe_ffn",
    )(block_expert, n_used, buf, w1, w3, w2)


def _combine_body(dest_ref, x_ref, wts_ref, mod_ref, fnw_ref, y_hbm, o_ref, rows_ref, sem, *, tc, final):
    def start(r, c):
        pltpu.make_async_copy(y_hbm.at[pl.ds(dest_ref[0, r], 1)], rows_ref.at[0, pl.ds(r, 1)], sem).start()
        pltpu.make_async_copy(y_hbm.at[pl.ds(dest_ref[1, r], 1)], rows_ref.at[1, pl.ds(r, 1)], sem).start()
        return c

    lax.fori_loop(0, tc, start, 0)

    def wait(r, c):
        pltpu.make_async_copy(y_hbm.at[pl.ds(0, 1)], rows_ref.at[0, pl.ds(0, 1)], sem).wait()
        pltpu.make_async_copy(y_hbm.at[pl.ds(0, 1)], rows_ref.at[1, pl.ds(0, 1)], sem).wait()
        return c

    lax.fori_loop(0, tc, wait, 0)
    eye = jnp.where(lax.broadcasted_iota(I32, (tc, tc), 0) == lax.broadcasted_iota(I32, (tc, tc), 1), 1.0, 0.0)
    wcol = _nt(eye, wts_ref[...], precision=HIGHEST)
    y = wcol[:, 0:1] * rows_ref[0] + wcol[:, 1:2] * rows_ref[1]
    out = x_ref[...] + mod_ref[5:6, :] * y
    if final:
        ms = jnp.mean(out * out, axis=-1, keepdims=True)
        out = out * lax.rsqrt(ms + EPS) * fnw_ref[...]
    o_ref[...] = out


def _moe_combine(x, dest, wts, mod, fnw, y_sorted, col0, final):
    bsz, t_len, _ = x.shape
    tc = min(256, t_len)
    nt = t_len // tc
    c0 = col0 // tc
    return pl.pallas_call(
        functools.partial(_combine_body, tc=tc, final=final),
        grid=(bsz, nt),
        in_specs=[pl.BlockSpec((2, tc), lambda b, i: (0, c0 + b * nt + i), memory_space=pltpu.SMEM),
                  pl.BlockSpec((None, tc, D_MODEL), lambda b, i: (b, i, 0)),
                  pl.BlockSpec((8, tc), lambda b, i: (0, c0 + b * nt + i)),
                  pl.BlockSpec((None, 8, D_MODEL), lambda b, i: (b, 0, 0)),
                  pl.BlockSpec((1, D_MODEL), lambda b, i: (0, 0)),
                  pl.BlockSpec(memory_space=pl.ANY)],
        out_specs=pl.BlockSpec((None, tc, D_MODEL), lambda b, i: (b, i, 0)),
        out_shape=jax.ShapeDtypeStruct(x.shape, F32),
        scratch_shapes=[pltpu.VMEM((2, tc, D_MODEL), F32), pltpu.SemaphoreType.DMA],
        compiler_params=_params("arbitrary", "arbitrary"),
        name="moe_combine",
    )(dest, x, wts, mod, fnw, y_sorted)


def _moe(f_list, ids, wts, w1, w3, w2):
    n = ids.shape[1]
    rank, counts = _moe_rank(ids)
    counts = counts[:, 0]
    padded = (counts + MOE_ROWS - 1) // MOE_ROWS * MOE_ROWS
    pad_ends = jnp.cumsum(padded)
    pad_starts = pad_ends - padded
    onehot = ids[None, :2] == jnp.arange(N_EXPERTS, dtype=I32)[:, None, None]
    dest = jnp.sum(jnp.where(onehot, pad_starts[:, None, None], 0), axis=0) + rank[:2]
    nblk = (2 * n) // MOE_ROWS + N_EXPERTS
    block_start = jnp.arange(nblk, dtype=I32) * MOE_ROWS
    block_expert = jnp.minimum(jnp.sum(block_start[:, None] >= pad_ends[None, :], axis=-1), N_EXPERTS - 1)
    n_used = (pad_ends[-1:] // MOE_ROWS).astype(I32)
    buf = jnp.zeros((nblk * MOE_ROWS, D_MODEL), F32)
    col = 0
    for f in f_list:
        f2d = f.reshape(-1, D_MODEL)
        buf = _moe_dispatch(f2d, lax.slice_in_dim(dest, col, col + f2d.shape[0], axis=1), buf)
        col += f2d.shape[0]
    y_sorted = _moe_ffn(buf, block_expert.astype(I32), n_used, w1, w3, w2)
    return y_sorted, dest


def _expand_heads(a, groups):
    idx = np.concatenate([np.arange(HEAD) + (h * groups // N_HEADS) * HEAD for h in range(N_HEADS)])
    return a[..., idx]


def kernel(x, c, ctx, c_ctx, ada_w, ada_b, norm_mix_w, norm_ffn_w, w_in, conv_a_w, conv_a_b, hgrn_lb,
           hgrn_norm_w, ssm_conv_w, ssm_conv_b, ssm_A_log, ssm_dt_bias, ssm_D, ssm_norm_w, w_branch, w_out,
           router_group_w, router_group_b, router_expert_w, router_expert_b, moe_w1, moe_w3, moe_w2,
           final_norm_w):
    depth = ada_w.shape[0]
    bsz, t_len, _ = x.shape
    t_ctx = ctx.shape[1]
    n_lat = bsz * t_len

    cond = jnp.concatenate([c, c_ctx[None, :], jnp.zeros((8 - bsz - 1, D_MODEL), F32)], axis=0)
    mods = _ada(cond, ada_w, ada_b)
    sc = {k: jnp.asarray(v) for k, v in _scan_constants().items()}
    fft_l = tuple(jnp.asarray(a) for a in _fft_constants(t_len))
    fft_c = tuple(jnp.asarray(a) for a in _fft_constants(t_ctx))
    e64 = sc["ebd"].astype(BF16)
    lane128 = np.arange(SLAB) // (2 * HEAD)
    e128 = jnp.asarray(lane128[:, None] == lane128[None, :], BF16)
    zero_state = jnp.zeros((bsz, 4, SLAB, SLAB), F32)
    fnw = final_norm_w.reshape(1, D_MODEL)

    for l in range(depth):
        last = l == depth - 1
        six = mods[l].reshape(8, 6, D_MODEL)
        mod_l = jnp.pad(six[:bsz], ((0, 0), (0, 2), (0, 0)))
        mod_c = jnp.broadcast_to(jnp.pad(six[bsz], ((0, 2), (0, 0))), (bsz, 8, D_MODEL))

        wl = w_in[l]
        xbc0 = 10 * SLAB
        w_b = _expand_heads(wl[:, xbc0 + SLAB:xbc0 + SLAB + 2 * HEAD], 2)
        w_c = _expand_heads(wl[:, xbc0 + SLAB + 2 * HEAD:xbc0 + 2 * SLAB], 2)
        dt0 = 12 * SLAB
        w_dt = jnp.repeat(wl[:, dt0:dt0 + 2 * N_HEADS], HEAD, axis=1)
        w_ext = jnp.concatenate([wl[:, :xbc0 + SLAB], w_b, w_c, w_dt], axis=1).astype(BF16)
        w_gate = wl[:, dt0 + 2 * N_HEADS:].astype(BF16)
        cw = ssm_conv_w[l]
        scw = jnp.concatenate([cw[:, :SLAB], _expand_heads(cw[:, SLAB:SLAB + 2 * HEAD], 2),
                               _expand_heads(cw[:, SLAB + 2 * HEAD:], 2)], axis=1)
        cbias = ssm_conv_b[l][None, :]
        scb = jnp.concatenate([cbias[:, :SLAB], _expand_heads(cbias[:, SLAB:SLAB + 2 * HEAD], 2),
                               _expand_heads(cbias[:, SLAB + 2 * HEAD:], 2)], axis=1)
        caw, cab = conv_a_w[l], conv_a_b[l][None, :]
        nmix, nffn = norm_mix_w[l][None, :], norm_ffn_w[l][None, :]
        alog_lane = jnp.repeat(ssm_A_log[l], HEAD, axis=1)
        dtbias_lane = jnp.repeat(ssm_dt_bias[l], HEAD, axis=1)
        hnw = jnp.tile(hgrn_norm_w[l], N_HEADS)[None, :]
        sd = jnp.repeat(ssm_D[l], HEAD)[None, :]
        snw = ssm_norm_w[l][None, :]
        wb, wo = w_branch[l].astype(BF16), w_out[l].astype(BF16)
        wr = jnp.concatenate([router_group_w[l].T, jnp.zeros((8 - N_GROUPS, D_MODEL), F32),
                              router_expert_w[l].T], axis=0)
        br = jnp.concatenate([router_group_b[l], jnp.zeros((8 - N_GROUPS,), F32),
                              router_expert_b[l]])[:, None]
        merge_w = (nmix, nffn, w_gate, wb, wo, hnw, sd, snw, e64, e128, wr, br)

        p3_c = _inproj(ctx, mod_c, nmix, w_ext, caw, cab, scw, scb)
        *scans_c, states = _scan(p3_c, hgrn_lb, alog_lane, dtbias_lane, zero_state, sc, l)
        p3_l = _inproj(x, mod_l, nmix, w_ext, caw, cab, scw, scb)
        *scans_l, _ = _scan(p3_l, hgrn_lb, alog_lane, dtbias_lane, states, sc, l)
        yc_l = _fourier(p3_l, fft_l)
        x, f_l, ids, wts = _merge(x, p3_l, scans_l, yc_l, mod_l, *merge_w)
        f_list = [f_l]
        if not last:
            yc_c = _fourier(p3_c, fft_c)
            ctx, f_c, ids_c, wts_c = _merge(ctx, p3_c, scans_c, yc_c, mod_c, *merge_w)
            f_list.append(f_c)
            ids = jnp.concatenate([ids, ids_c], axis=1)
            wts = jnp.concatenate([wts, wts_c], axis=1)
        y_sorted, dest = _moe(f_list, ids, wts, moe_w1[l], moe_w3[l], moe_w2[l])
        x = _moe_combine(x, dest, wts, mod_l, fnw, y_sorted, 0, last)
        if not last:
            ctx = _moe_combine(ctx, dest, wts, mod_c, fnw, y_sorted, n_lat, False)
    return x
```

```python
import functools

import numpy as np
import jax
import jax.numpy as jnp
from jax import lax
from jax.experimental import pallas as pl
from jax.experimental.pallas import tpu as pltpu

F32, BF16, I32 = jnp.float32, jnp.bfloat16, jnp.int32
HIGHEST = lax.Precision.HIGHEST
EPS = 1e-6

D_MODEL = 1024
SLAB = 256
N_HEADS = 4
HEAD = 64
C_GDIM = 64
N_BRANCH = 4
CHUNK = 64
N_LEVELS = 6
N_GROUPS, EXPERTS_PER_GROUP = 4, 8
N_EXPERTS = N_GROUPS * EXPERTS_PER_GROUP
D_FF = 512
ROUTER_ROWS = 8 + N_EXPERTS
MOE_ROWS = 256
VMEM_LIMIT = 56 * 1024 * 1024

S_YA, S_Q, S_FF, S_FB, S_I, S_G, S_FOUR, S_Z, S_XS, S_BC, S_DTF, S_DTB = range(12)
N_SLABS = 12


def _sigmoid(x):
    return 1.0 / (1.0 + jnp.exp(-x))


def _softplus(x):
    return jnp.maximum(x, 0.0) + jnp.log1p(jnp.exp(-jnp.abs(x)))


def _tile(n, pref):
    t = pref
    while n % t:
        t //= 2
    return t


def _params(*sem):
    return pltpu.CompilerParams(dimension_semantics=sem, vmem_limit_bytes=VMEM_LIMIT)


def _nt(a, b, **kw):
    return lax.dot_general(a, b, (((1,), (1,)), ((), ())), preferred_element_type=F32, **kw)


def _tn(a, b, **kw):
    return lax.dot_general(a, b, (((0,), (0,)), ((), ())), preferred_element_type=F32, **kw)


def _ada_body(s_ref, w_ref, b_ref, o_ref):
    s = s_ref[...]
    s = s * _sigmoid(s)
    o_ref[...] = jnp.dot(s, w_ref[...], precision=HIGHEST, preferred_element_type=F32) + b_ref[...]


def _ada(cond, ada_w, ada_b):
    depth = ada_w.shape[0]
    n6 = ada_w.shape[2] // D_MODEL
    return pl.pallas_call(
        _ada_body,
        grid=(depth, n6),
        in_specs=[pl.BlockSpec((8, D_MODEL), lambda l, j: (0, 0)),
                  pl.BlockSpec((None, D_MODEL, D_MODEL), lambda l, j: (l, 0, j)),
                  pl.BlockSpec((None, 1, D_MODEL), lambda l, j: (l, 0, j))],
        out_specs=pl.BlockSpec((None, 8, D_MODEL), lambda l, j: (l, 0, j)),
        out_shape=jax.ShapeDtypeStruct((depth, 8, n6 * D_MODEL), F32),
        compiler_params=_params("parallel", "parallel"),
        name="ada_mod",
    )(cond, ada_w, ada_b.reshape(depth, 1, -1))


def _norm_mod(x, nw, shift, scale):
    ms = jnp.mean(x * x, axis=-1, keepdims=True)
    return (x * lax.rsqrt(ms + EPS) * nw) * (1.0 + scale) + shift


def _inproj_body(x_ref, xp_ref, xn_ref, mod_ref, nw_ref, w_ref, caw_ref, cab_ref, scw_ref, scb_ref,
                 p_ref, *, tm, nt):
    i = pl.program_id(1)
    xe = jnp.concatenate([xp_ref[...], x_ref[...], xn_ref[...]], axis=0)
    h = _norm_mod(xe, nw_ref[...], mod_ref[0:1, :], mod_ref[1:2, :])
    pr = jnp.dot(h.astype(BF16), w_ref[...], preferred_element_type=F32)
    rows = lax.broadcasted_iota(I32, (tm + 16, 1), 0)
    lo = jnp.where(i > 0, 0, 8)
    hi = jnp.where(i < nt - 1, tm + 16, tm + 8)
    vm = jnp.where((rows >= lo) & (rows < hi), 1.0, 0.0)

    def conv3(u, w, b):
        u = u * vm
        return u[7:tm + 7] * w[0:1, :] + u[8:tm + 8] * w[1:2, :] + u[9:tm + 9] * w[2:3, :] + b[...]

    ca = conv3(pr[:, SLAB:2 * SLAB] * pr[:, 2 * SLAB:3 * SLAB], caw_ref, cab_ref)
    p_ref[S_YA] = pr[8:tm + 8, 0:SLAB] * ca
    for s in range(7):
        p_ref[S_Q + s] = pr[8:tm + 8, (3 + s) * SLAB:(4 + s) * SLAB]
    cs = conv3(pr[:, 10 * SLAB:12 * SLAB], scw_ref, scb_ref)
    cs = cs * _sigmoid(cs)
    p_ref[S_XS] = cs[:, 0:SLAB]
    p_ref[S_BC] = cs[:, SLAB:2 * SLAB]
    p_ref[S_DTF] = pr[8:tm + 8, 12 * SLAB:13 * SLAB]
    p_ref[S_DTB] = pr[8:tm + 8, 13 * SLAB:14 * SLAB]


def _inproj(x, mod, nw, w_ext, caw, cab, scw, scb):
    bsz, t_len, _ = x.shape
    tm = min(512, t_len)
    nt = t_len // tm
    t8 = tm // 8
    full = lambda shape: pl.BlockSpec(shape, lambda b, i: (0,) * len(shape))
    return pl.pallas_call(
        functools.partial(_inproj_body, tm=tm, nt=nt),
        grid=(bsz, nt),
        in_specs=[pl.BlockSpec((None, tm, D_MODEL), lambda b, i: (b, i, 0)),
                  pl.BlockSpec((None, 8, D_MODEL), lambda b, i: (b, jnp.maximum(i * t8 - 1, 0), 0)),
                  pl.BlockSpec((None, 8, D_MODEL), lambda b, i: (b, jnp.minimum((i + 1) * t8, t_len // 8 - 1), 0)),
                  pl.BlockSpec((None, 8, D_MODEL), lambda b, i: (b, 0, 0)),
                  full((1, D_MODEL)), full(w_ext.shape), full(caw.shape), full(cab.shape),
                  full(scw.shape), full(scb.shape)],
        out_specs=pl.BlockSpec((N_SLABS, None, tm, SLAB), lambda b, i: (0, b, i, 0)),
        out_shape=jax.ShapeDtypeStruct((N_SLABS, bsz, t_len, SLAB), F32),
        compiler_params=_params("parallel", "parallel"),
        name="in_proj",
    )(x, x, x, mod, nw, w_ext, caw, cab, scw, scb)


def _scan_constants():
    t = np.arange(CHUNK)[:, None]
    r = np.arange(CHUNK)[None, :]
    out = {}
    for name, fwd in (("f", True), ("b", False)):
        blocks = [(r <= t) if fwd else (r >= t), (r > t) if fwd else (r < t)]
        qs, ks, masks = [], [], []
        for lvl in range(N_LEVELS):
            m = 1 << lvl
            blk = t // (2 * m)
            ref = blk * 2 * m + m - 1
            upper = (t % (2 * m)) >= m
            s_blk = (r // (2 * m))
            s_upper = (r % (2 * m)) >= m
            if fwd:
                qs.append(upper & (r > ref) & (r <= t))
                ks.append((~upper) & (r > t) & (r <= ref))
                masks.append(upper & (~s_upper) & (s_blk == blk))
            else:
                qs.append((~upper) & (r >= t) & (r <= ref))
                ks.append(upper & (r > ref) & (r < t))
                masks.append((~upper) & s_upper & (s_blk == blk))
        out["cm_" + name] = np.concatenate(blocks + [q | k for q, k in zip(qs, ks)], axis=0).astype(np.float32)
        out["lm_" + name] = np.stack([np.tile(mk, (1, N_HEADS)) for mk in masks]).astype(np.float32)
        out["su_" + name] = np.tile((r.T > r) if fwd else (r.T < r), (1, N_HEADS)).astype(np.float32)
        out["sc_" + name] = np.tile((t >= r) if fwd else (r >= t), (1, N_HEADS)).astype(np.float32)
    lane_head = np.arange(SLAB) // HEAD
    out["ebd"] = (lane_head[:, None] == lane_head[None, :]).astype(np.float32)
    lane_group = np.arange(2 * HEAD) // HEAD
    out["gm4"] = (lane_head[:, None] // 2 == lane_group[None, :]).astype(np.float32)
    return out


def _split_bf16(a):
    hi = a.astype(BF16)
    return hi, (a - hi.astype(F32)).astype(BF16)


def _head_blocks(a, mask):
    return jnp.concatenate([a.astype(BF16)] * N_HEADS, axis=0) * mask


def _hgrn_chunk(qv, kv, vv, logf, s_ref, k, cm, lm_ref, ebd, fwd):
    c = CHUNK
    e = jnp.exp(jnp.dot(cm, jnp.concatenate(_split_bf16(logf), axis=0), preferred_element_type=F32))
    eb = e.astype(BF16)
    qb, kb = qv.astype(BF16), kv.astype(BF16)
    p = jnp.zeros((c, SLAB), F32)
    for lvl in range(N_LEVELS):
        el = eb[(2 + lvl) * c:(3 + lvl) * c]
        p = p + _nt(qb * el, _head_blocks(kb * el, ebd)) * lm_ref[lvl]
    state = s_ref[k]
    o = jnp.dot((qv * kv).astype(BF16), ebd, preferred_element_type=F32) * vv
    o = o + jnp.dot(p.astype(BF16), _head_blocks(vv, ebd), preferred_element_type=F32)
    o = o + _nt((qv * e[0:c]).astype(BF16), state.astype(BF16))
    upd = _tn(vv.astype(BF16), (kv * e[c:2 * c]).astype(BF16))
    tot = e[c - 1:c] if fwd else e[0:1]
    s_ref[k] = state * tot + upd * ebd.astype(F32)
    return o


def _ssd_chunk(bc, xv, dt, a, s_ref, k, sm, um, caus, gm4, gms, ebd, fwd):
    c = CHUNK
    bv, cv = bc[:, :2 * HEAD], bc[:, 2 * HEAD:].astype(BF16)
    a_hi, a_lo = _split_bf16(a)
    ex = jnp.dot(sm, jnp.concatenate([a_hi, a_lo], axis=0), preferred_element_type=F32)
    cum, aft = ex[0:c], ex[c:]
    between = jnp.dot(sm[0:c], jnp.concatenate([a_hi * um, a_lo * um], axis=0),
                      preferred_element_type=F32)
    p = _nt(cv, _head_blocks(bv, gm4)) * (jnp.exp(between) * caus)
    xt = xv * dt
    state = s_ref[k]
    o = jnp.dot(p.astype(BF16), _head_blocks(xt, ebd), preferred_element_type=F32)
    o = o + jnp.dot(cv, state.astype(BF16), preferred_element_type=F32) * jnp.exp(cum)
    upd = _tn(bv.astype(BF16), (xt * jnp.exp(aft)).astype(BF16))
    tot = jnp.exp(cum[c - 1:c] if fwd else cum[0:1])
    s_ref[k] = state * tot + upd * gms
    return o


def _scan_body(qf, ff, vf, xsf, bcf, dtf, qb, fb, vb, xsb, bcb, dtb,
               lb_ref, alog_ref, dtbias_ref, cmf_ref, cmb_ref, lmf_ref, lmb_ref, suf_ref, sub_ref, scf_ref, scb_ref,
               gm4_ref, gms_ref, ebd_ref, h0_ref, d0_ref,
               ohf, ohb, osf, osb, hout_ref, dout_ref, sh_ref, sd_ref, *, layer, nck, nsteps):
    i = pl.program_id(1)

    @pl.when(i == 0)
    def _():
        sh_ref[...] = h0_ref[...]
        sd_ref[...] = d0_ref[...]

    ebd = ebd_ref[...]
    cmf, cmb = cmf_ref[...], cmb_ref[...]
    smf, smb = cmf_ref[0:2 * CHUNK, :], cmb_ref[0:2 * CHUNK, :]

    def lower_bound(d):
        rows = lb_ref[d]
        ex = jnp.exp(rows - jnp.max(rows, axis=0, keepdims=True))
        prob = ex / jnp.sum(ex, axis=0, keepdims=True)
        lb = jnp.zeros((1, SLAB), F32)
        for j in range(1, layer + 1):
            lb = lb + prob[j:j + 1]
        return lb

    lbs = [lower_bound(0), lower_bound(1)]

    def hgrn_inputs(q_raw, f_raw, lb):
        q = q_raw * _sigmoid(q_raw)
        log_sig = jnp.minimum(f_raw, 0.0) - jnp.log1p(jnp.exp(-jnp.abs(f_raw)))
        a = jnp.log(lb)
        b = jnp.log1p(-lb) + log_sig
        logf = jnp.maximum(a, b) + jnp.log1p(jnp.exp(-jnp.abs(a - b)))
        kk = (1.0 - lb) * _sigmoid(-f_raw)
        return q, kk, logf

    def ssd_inputs(dt_raw, d):
        dt = _softplus(dt_raw + dtbias_ref[d:d + 1, :])
        return dt, -jnp.exp(alog_ref[d:d + 1, :]) * dt

    def body(j, carry):
        rf = pl.ds(pl.multiple_of(j * CHUNK, CHUNK), CHUNK)
        rb = pl.ds(pl.multiple_of((nck - 1 - j) * CHUNK, CHUNK), CHUNK)
        q, kk, logf = hgrn_inputs(qf[rf, :], ff[rf, :], lbs[0])
        ohf[rf, :] = _hgrn_chunk(q, kk, vf[rf, :], logf, sh_ref, 0, cmf, lmf_ref, ebd, True)
        q, kk, logf = hgrn_inputs(qb[rb, :], fb[rb, :], lbs[1])
        ohb[rb, :] = _hgrn_chunk(q, kk, vb[rb, :], logf, sh_ref, 1, cmb, lmb_ref, ebd, False)
        dt, a = ssd_inputs(dtf[rf, :], 0)
        osf[rf, :] = _ssd_chunk(bcf[rf, :], xsf[rf, :], dt, a, sd_ref, 0, smf, suf_ref[...], scf_ref[...],
                                gm4_ref[...], gms_ref[...], ebd, True)
        dt, a = ssd_inputs(dtb[rb, :], 1)
        osb[rb, :] = _ssd_chunk(bcb[rb, :], xsb[rb, :], dt, a, sd_ref, 1, smb, sub_ref[...], scb_ref[...],
                                gm4_ref[...], gms_ref[...], ebd, False)
        return carry

    lax.fori_loop(0, nck, body, 0, unroll=2)

    @pl.when(i == nsteps - 1)
    def _():
        hout_ref[...] = sh_ref[...]
        dout_ref[...] = sd_ref[...]


def _scan(p3, hgrn_lb, alog_lane, dtbias_lane, s0, consts, layer):
    _, bsz, t_len, _ = p3.shape
    cb = min(256, t_len)
    nb = t_len // cb
    slab_f = lambda s: pl.BlockSpec((None, None, cb, SLAB), lambda b, i: (s, b, i, 0))
    slab_b = lambda s: pl.BlockSpec((None, None, cb, SLAB), lambda b, i: (s, b, nb - 1 - i, 0))
    full = lambda a: pl.BlockSpec(a.shape, lambda b, i: (0,) * a.ndim)
    bf = lambda name: consts[name].astype(BF16)
    cm2 = lambda name: jnp.concatenate([bf(name), bf(name)], axis=1)
    small = [hgrn_lb, alog_lane, dtbias_lane, cm2("cm_f"), cm2("cm_b"), consts["lm_f"], consts["lm_b"],
             bf("su_f"), bf("su_b"), consts["sc_f"], consts["sc_b"], bf("gm4"), consts["gm4"].T, bf("ebd")]
    o_f = pl.BlockSpec((None, cb, SLAB), lambda b, i: (b, i, 0))
    o_b = pl.BlockSpec((None, cb, SLAB), lambda b, i: (b, nb - 1 - i, 0))
    st_h = pl.BlockSpec((None, 2, SLAB, SLAB), lambda b, i: (b, 0, 0, 0))
    st_d = pl.BlockSpec((None, 2, 2 * HEAD, SLAB), lambda b, i: (b, 0, 0, 0))
    o_shape = jax.ShapeDtypeStruct((bsz, t_len, SLAB), F32)
    h0, d0 = s0
    return pl.pallas_call(
        functools.partial(_scan_body, layer=layer, nck=cb // CHUNK, nsteps=nb),
        grid=(bsz, nb),
        in_specs=([slab_f(s) for s in (S_Q, S_FF, S_I, S_XS, S_BC, S_DTF)]
                  + [slab_b(s) for s in (S_Q, S_FB, S_I, S_XS, S_BC, S_DTB)]
                  + [full(a) for a in small] + [st_h, st_d]),
        out_specs=[o_f, o_b, o_f, o_b, st_h, st_d],
        out_shape=[o_shape, o_shape, o_shape, o_shape, jax.ShapeDtypeStruct(h0.shape, F32),
                   jax.ShapeDtypeStruct(d0.shape, F32)],
        scratch_shapes=[pltpu.VMEM((2, SLAB, SLAB), F32), pltpu.VMEM((2, 2 * HEAD, SLAB), F32)],
        compiler_params=_params("parallel", "arbitrary"),
        name="chunk_scan",
    )(*([p3] * 12), *small, h0, d0)


def _fft_sizes(t_len):
    tf = 64 if t_len >= 4096 else 16
    return tf, t_len // tf


def _fft_constants(t_len):
    tf, ts = _fft_sizes(t_len)
    c = np.arange(SLAB)
    same = (c[:, None] // C_GDIM) == (c[None, :] // C_GDIM)
    ang = 2 * np.pi * ((c[:, None] % C_GDIM) * (c[None, :] % C_GDIM) % C_GDIM) / C_GDIM
    chan = np.concatenate([np.cos(ang) * same, -np.sin(ang) * same], axis=1)
    k2 = np.arange(ts)
    a = 2 * np.pi * (k2[:, None] * k2[None, :] % ts) / ts
    flat = np.block([[np.cos(a), np.sin(a)], [-np.sin(a), np.cos(a)]])
    k1 = np.arange(tf)[None, :, None]
    tfi = np.arange(tf)[None, None, :]
    kk2 = np.arange(ts)[:, None, None]
    th = 2 * np.pi * (((tfi * k1 * ts) + tfi * kk2) % t_len) / t_len
    scale = 1.0 / np.sqrt(t_len * C_GDIM)
    slab = np.concatenate([np.cos(th), np.sin(th)], axis=2) * scale
    return chan.astype(np.float32), flat.astype(np.float32), slab.astype(np.float32)


def _fft_flat_body(x_ref, chan_ref, flat_ref, y_ref, *, ts, groups):
    for g in range(groups):
        xg = x_ref[:, g * SLAB:(g + 1) * SLAB]
        a = jnp.dot(xg, chan_ref[...], precision=HIGHEST, preferred_element_type=F32)
        z = jnp.concatenate([a[:, :SLAB], a[:, SLAB:]], axis=0)
        y = jnp.dot(flat_ref[...], z, precision=HIGHEST, preferred_element_type=F32)
        y_ref[0, :, g * SLAB:(g + 1) * SLAB] = y[:ts]
        y_ref[1, :, g * SLAB:(g + 1) * SLAB] = y[ts:]


def _fft_slab_body(y_ref, m_ref, o_ref, *, kb):
    for j in range(kb):
        z = jnp.concatenate([y_ref[0, j], y_ref[1, j]], axis=0)
        o_ref[j] = jnp.dot(m_ref[j], z, precision=HIGHEST, preferred_element_type=F32)


def _fourier(p3, consts):
    _, bsz, t_len, _ = p3.shape
    tf, ts = _fft_sizes(t_len)
    chan, flat, slab = consts
    groups = min(8, tf)
    lanes = groups * SLAB
    y = pl.pallas_call(
        functools.partial(_fft_flat_body, ts=ts, groups=groups),
        grid=(bsz, tf // groups),
        in_specs=[pl.BlockSpec((None, ts, lanes), lambda b, j: (b, 0, j)),
                  pl.BlockSpec(chan.shape, lambda b, j: (0, 0)),
                  pl.BlockSpec(flat.shape, lambda b, j: (0, 0))],
        out_specs=pl.BlockSpec((None, 2, ts, lanes), lambda b, j: (b, 0, 0, j)),
        out_shape=jax.ShapeDtypeStruct((bsz, 2, ts, tf * SLAB), F32),
        compiler_params=_params("parallel", "parallel"),
        name="fft_flat",
    )(p3[S_FOUR].reshape(bsz, ts, tf * SLAB), chan, flat)
    kb = min(8, ts)
    out = pl.pallas_call(
        functools.partial(_fft_slab_body, kb=kb),
        grid=(bsz, ts // kb),
        in_specs=[pl.BlockSpec((None, 2, kb, tf, SLAB), lambda b, j: (b, 0, j, 0, 0)),
                  pl.BlockSpec((kb, tf, 2 * tf), lambda b, j: (j, 0, 0))],
        out_specs=pl.BlockSpec((None, kb, tf, SLAB), lambda b, j: (b, j, 0, 0)),
        out_shape=jax.ShapeDtypeStruct((bsz, ts, tf, SLAB), F32),
        compiler_params=_params("parallel", "parallel"),
        name="fft_slab",
    )(y.reshape(bsz, 2, ts, tf, SLAB), slab)
    return out.transpose(0, 2, 1, 3).reshape(bsz, t_len, SLAB)


def _merge_body(x_ref, ya_ref, g_ref, z_ref, xs_ref, hf_ref, hb_ref, sf_ref, sb_ref, yc_ref,
                mod_ref, nmix_ref, nffn_ref, wg_ref, wb_ref, wo_ref, hnw_ref, sd_ref, snw_ref,
                e64_ref, e128_ref, wr_ref, br_ref,
                xo_ref, f_ref, ids_ref, wts_ref):
    x = x_ref[...]
    h = _norm_mod(x, nmix_ref[...], mod_ref[0:1, :], mod_ref[1:2, :]).astype(BF16)

    def group_rms(v, e_ref, width):
        ms = jnp.dot((v * v).astype(BF16), e_ref[...], preferred_element_type=F32) * (1.0 / width)
        return v * lax.rsqrt(ms + EPS)

    g = g_ref[...]
    y_b = group_rms(hf_ref[...] + hb_ref[...], e64_ref, HEAD) * hnw_ref[...] * (g * _sigmoid(g))
    z = z_ref[...]
    y_d = (sf_ref[...] + sb_ref[...] + sd_ref[...] * xs_ref[...]) * (z * _sigmoid(z))
    y_d = group_rms(y_d, e128_ref, 2 * HEAD) * snw_ref[...]
    branches = (ya_ref[...], y_b, yc_ref[...], y_d)
    merged = None
    for kb in range(N_BRANCH):
        gate = _sigmoid(jnp.dot(h, wg_ref[:, kb * D_MODEL:(kb + 1) * D_MODEL], preferred_element_type=F32))
        term = gate * jnp.dot(branches[kb].astype(BF16), wb_ref[kb], preferred_element_type=F32)
        merged = term if merged is None else merged + term
    mix = jnp.dot(merged.astype(BF16), wo_ref[...], preferred_element_type=F32)
    xo = x + mod_ref[2:3, :] * mix
    xo_ref[...] = xo
    f = _norm_mod(xo, nffn_ref[...], mod_ref[3:4, :], mod_ref[4:5, :])
    f_ref[...] = f

    lg = _nt(wr_ref[...], f, precision=HIGHEST) + br_ref[...]
    gl = lg[0:N_GROUPS]
    gidx = lax.broadcasted_iota(I32, gl.shape, 0)
    gmax = jnp.max(gl, axis=0, keepdims=True)
    g_top = jnp.min(jnp.where(gl == gmax, gidx, N_GROUPS), axis=0, keepdims=True)
    p_group = 1.0 / jnp.sum(jnp.exp(gl - gmax), axis=0, keepdims=True)
    e_in = jnp.zeros((EXPERTS_PER_GROUP, gl.shape[1]), F32)
    for grp in range(N_GROUPS):
        e_in = jnp.where(g_top == grp, lg[8 + 8 * grp:16 + 8 * grp], e_in)
    eidx = lax.broadcasted_iota(I32, e_in.shape, 0)
    v1 = jnp.max(e_in, axis=0, keepdims=True)
    i1 = jnp.min(jnp.where(e_in == v1, eidx, EXPERTS_PER_GROUP), axis=0, keepdims=True)
    rest = jnp.where(eidx == i1, -jnp.inf, e_in)
    v2 = jnp.max(rest, axis=0, keepdims=True)
    i2 = jnp.min(jnp.where(rest == v2, eidx, EXPERTS_PER_GROUP), axis=0, keepdims=True)
    w1 = 1.0 / (1.0 + jnp.exp(v2 - v1))
    row = lax.broadcasted_iota(I32, (8, gl.shape[1]), 0)
    base = g_top * EXPERTS_PER_GROUP
    ids_ref[...] = jnp.where(row == 0, base + i1, jnp.where(row == 1, base + i2, 0))
    wts_ref[...] = jnp.where(row == 0, p_group * w1, jnp.where(row == 1, p_group * (1.0 - w1), 0.0))


def _merge(x, p3, scans, y_c, mod, nmix, nffn, wg, wb, wo, hnw, sd, snw, e64, e128, wr, br):
    bsz, t_len, _ = x.shape
    tm = min(512, t_len)
    nt = t_len // tm
    row = lambda w: pl.BlockSpec((None, tm, w), lambda b, i: (b, i, 0))
    slab = lambda s: pl.BlockSpec((None, None, tm, SLAB), lambda b, i: (s, b, i, 0))
    full = lambda a: pl.BlockSpec(a.shape, lambda b, i: (0,) * a.ndim, pipeline_mode=pl.Buffered(1))
    consts = [nmix, nffn, wg, wb, wo, hnw, sd, snw, e64, e128, wr, br]
    lane_out = pl.BlockSpec((8, tm), lambda b, i: (0, b * nt + i))
    return pl.pallas_call(
        _merge_body,
        grid=(bsz, nt),
        in_specs=([row(D_MODEL)] + [slab(s) for s in (S_YA, S_G, S_Z, S_XS)] + [row(SLAB)] * 5
                  + [pl.BlockSpec((None, 8, D_MODEL), lambda b, i: (b, 0, 0))] + [full(a) for a in consts]),
        out_specs=[row(D_MODEL), row(D_MODEL), lane_out, lane_out],
        out_shape=[jax.ShapeDtypeStruct(x.shape, F32), jax.ShapeDtypeStruct(x.shape, F32),
                   jax.ShapeDtypeStruct((8, bsz * t_len), I32), jax.ShapeDtypeStruct((8, bsz * t_len), F32)],
        compiler_params=_params("parallel", "parallel"),
        name="merge_router",
    )(x, p3, p3, p3, p3, *scans, y_c, mod, *consts)


def _rank_body(ids_ref, tri_ref, rank_ref, cnt_ref, carry_ref, *, nsteps):
    i = pl.program_id(0)

    @pl.when(i == 0)
    def _():
        carry_ref[...] = jnp.zeros_like(carry_ref)

    tr = ids_ref.shape[1]
    eidx = lax.broadcasted_iota(I32, (N_EXPERTS, tr), 0)
    oh0 = jnp.where(eidx == ids_ref[0:1, :], 1.0, 0.0)
    oh1 = jnp.where(eidx == ids_ref[1:2, :], 1.0, 0.0)
    both = oh0 + oh1
    before = jnp.dot(both.astype(BF16), tri_ref[...], preferred_element_type=F32)
    pos = before + carry_ref[...]
    r0 = jnp.sum(oh0 * pos, axis=0, keepdims=True)
    r1 = jnp.sum(oh1 * pos, axis=0, keepdims=True)
    row = lax.broadcasted_iota(I32, (8, tr), 0)
    rank_ref[...] = jnp.where(row == 0, r0, jnp.where(row == 1, r1, 0.0)).astype(I32)
    carry_ref[...] = carry_ref[...] + jnp.sum(both, axis=1, keepdims=True)

    @pl.when(i == nsteps - 1)
    def _():
        cnt_ref[...] = carry_ref[...].astype(I32)


def _moe_rank(ids):
    n = ids.shape[1]
    tr = _tile(n, 1024)
    tri = jnp.triu(jnp.ones((tr, tr), BF16), k=1)
    return pl.pallas_call(
        functools.partial(_rank_body, nsteps=n // tr),
        grid=(n // tr,),
        in_specs=[pl.BlockSpec((8, tr), lambda i: (0, i)), pl.BlockSpec((tr, tr), lambda i: (0, 0))],
        out_specs=[pl.BlockSpec((8, tr), lambda i: (0, i)), pl.BlockSpec((N_EXPERTS, 1), lambda i: (0, 0))],
        out_shape=[jax.ShapeDtypeStruct((8, n), I32), jax.ShapeDtypeStruct((N_EXPERTS, 1), I32)],
        scratch_shapes=[pltpu.VMEM((N_EXPERTS, 1), F32)],
        compiler_params=_params("arbitrary"),
        name="moe_rank",
    )(ids, tri)


def _row_copy(src_hbm, dst_hbm, sem, src_row, dst_row):
    return pltpu.make_async_copy(src_hbm.at[pl.ds(src_row, 1)], dst_hbm.at[pl.ds(dst_row, 1)], sem)


def _dispatch_body(dest_ref, f_ref, buf_in, buf_hbm, sem, *, tg):
    del buf_in

    def start(r, c):
        _row_copy(f_ref, buf_hbm, sem, r, dest_ref[0, r]).start(priority=0)
        _row_copy(f_ref, buf_hbm, sem, r, dest_ref[1, r]).start(priority=1)
        return c

    lax.fori_loop(0, tg, start, 0, unroll=8)
    for _ in range(2):
        pltpu.make_async_copy(f_ref, buf_hbm.at[pl.ds(0, tg)], sem).wait()


def _moe_dispatch(f2d, dest, buf):
    n = f2d.shape[0]
    tg = _tile(n, 512)
    return pl.pallas_call(
        functools.partial(_dispatch_body, tg=tg),
        grid=(n // tg,),
        in_specs=[pl.BlockSpec((2, tg), lambda i: (0, i), memory_space=pltpu.SMEM),
                  pl.BlockSpec((tg, D_MODEL), lambda i: (i, 0)), pl.BlockSpec(memory_space=pl.ANY)],
        out_specs=pl.BlockSpec(memory_space=pl.ANY),
        out_shape=jax.ShapeDtypeStruct(buf.shape, buf.dtype),
        scratch_shapes=[pltpu.SemaphoreType.DMA],
        input_output_aliases={2: 0},
        compiler_params=_params("arbitrary"),
        name="moe_dispatch",
    )(dest, f2d, buf)


def _ffn_body(be_ref, nu_ref, x_ref, w1_ref, w3_ref, w2_ref, o_ref, w1b_ref, w3b_ref, w2b_ref):
    i = pl.program_id(0)
    used = i < nu_ref[0]
    fresh = (i == 0) | (be_ref[i] != be_ref[jnp.maximum(i - 1, 0)])

    @pl.when(used & fresh)
    def _():
        w1b_ref[...] = w1_ref[...].astype(BF16)
        w3b_ref[...] = w3_ref[...].astype(BF16)
        w2b_ref[...] = w2_ref[...].astype(BF16)

    @pl.when(used)
    def _():
        xb = x_ref[...].astype(BF16)
        a = jnp.dot(xb, w1b_ref[...], preferred_element_type=F32)
        b = jnp.dot(xb, w3b_ref[...], preferred_element_type=F32)
        hid = (a * _sigmoid(a) * b).astype(BF16)
        o_ref[...] = jnp.dot(hid, w2b_ref[...], preferred_element_type=F32)

    @pl.when(jnp.logical_not(used))
    def _():
        o_ref[...] = jnp.zeros_like(o_ref)


def _moe_ffn(buf, block_expert, n_used, w1, w3, w2):
    nblk = buf.shape[0] // MOE_ROWS
    wspec = lambda shape: pl.BlockSpec((None,) + shape, lambda i, be, nu: (be[i], 0, 0))
    return pl.pallas_call(
        _ffn_body,
        grid_spec=pltpu.PrefetchScalarGridSpec(
            num_scalar_prefetch=2,
            grid=(nblk,),
            in_specs=[pl.BlockSpec((MOE_ROWS, D_MODEL), lambda i, be, nu: (i, 0)),
                      wspec((D_MODEL, D_FF)), wspec((D_MODEL, D_FF)), wspec((D_FF, D_MODEL))],
            out_specs=pl.BlockSpec((MOE_ROWS, D_MODEL), lambda i, be, nu: (i, 0)),
            scratch_shapes=[pltpu.VMEM((D_MODEL, D_FF), BF16), pltpu.VMEM((D_MODEL, D_FF), BF16),
                            pltpu.VMEM((D_FF, D_MODEL), BF16)]),
        out_shape=jax.ShapeDtypeStruct(buf.shape, F32),
        compiler_params=_params("arbitrary"),
        name="moe_ffn",
    )(block_expert, n_used, buf, w1, w3, w2)


def _combine_body(dest_ref, x_ref, wts_ref, mod_ref, fnw_ref, y_hbm, o_ref, rows_ref, sem, *, tc, final):
    def start(r, c):
        pltpu.make_async_copy(y_hbm.at[pl.ds(dest_ref[0, r], 1)], rows_ref.at[0, pl.ds(r, 1)], sem).start(priority=0)
        pltpu.make_async_copy(y_hbm.at[pl.ds(dest_ref[1, r], 1)], rows_ref.at[1, pl.ds(r, 1)], sem).start(priority=1)
        return c

    lax.fori_loop(0, tc, start, 0, unroll=8)
    for k in range(2):
        pltpu.make_async_copy(y_hbm.at[pl.ds(0, tc)], rows_ref.at[k], sem).wait()
    eye = jnp.where(lax.broadcasted_iota(I32, (tc, tc), 0) == lax.broadcasted_iota(I32, (tc, tc), 1), 1.0, 0.0)
    wcol = _nt(eye, wts_ref[...], precision=HIGHEST)
    y = wcol[:, 0:1] * rows_ref[0] + wcol[:, 1:2] * rows_ref[1]
    out = x_ref[...] + mod_ref[5:6, :] * y
    if final:
        ms = jnp.mean(out * out, axis=-1, keepdims=True)
        out = out * lax.rsqrt(ms + EPS) * fnw_ref[...]
    o_ref[...] = out


def _moe_combine(x, dest, wts, mod, fnw, y_sorted, col0, final):
    bsz, t_len, _ = x.shape
    tc = min(256, t_len)
    nt = t_len // tc
    c0 = col0 // tc
    return pl.pallas_call(
        functools.partial(_combine_body, tc=tc, final=final),
        grid=(bsz, nt),
        in_specs=[pl.BlockSpec((2, tc), lambda b, i: (0, c0 + b * nt + i), memory_space=pltpu.SMEM),
                  pl.BlockSpec((None, tc, D_MODEL), lambda b, i: (b, i, 0)),
                  pl.BlockSpec((8, tc), lambda b, i: (0, c0 + b * nt + i)),
                  pl.BlockSpec((None, 8, D_MODEL), lambda b, i: (b, 0, 0)),
                  pl.BlockSpec((1, D_MODEL), lambda b, i: (0, 0)),
                  pl.BlockSpec(memory_space=pl.ANY)],
        out_specs=pl.BlockSpec((None, tc, D_MODEL), lambda b, i: (b, i, 0)),
        out_shape=jax.ShapeDtypeStruct(x.shape, F32),
        scratch_shapes=[pltpu.VMEM((2, tc, D_MODEL), F32), pltpu.SemaphoreType.DMA],
        compiler_params=_params("arbitrary", "arbitrary"),
        name="moe_combine",
    )(dest, x, wts, mod, fnw, y_sorted)


def _moe(f_list, ids, wts, w1, w3, w2):
    n = ids.shape[1]
    rank, counts = _moe_rank(ids)
    counts = counts[:, 0]
    padded = (counts + MOE_ROWS - 1) // MOE_ROWS * MOE_ROWS
    pad_ends = jnp.cumsum(padded)
    pad_starts = pad_ends - padded
    onehot = ids[None, :2] == jnp.arange(N_EXPERTS, dtype=I32)[:, None, None]
    dest = jnp.sum(jnp.where(onehot, pad_starts[:, None, None], 0), axis=0) + rank[:2]
    nblk = (2 * n) // MOE_ROWS + N_EXPERTS
    block_start = jnp.arange(nblk, dtype=I32) * MOE_ROWS
    block_expert = jnp.minimum(jnp.sum(block_start[:, None] >= pad_ends[None, :], axis=-1), N_EXPERTS - 1)
    n_used = (pad_ends[-1:] // MOE_ROWS).astype(I32)
    buf = jnp.zeros((nblk * MOE_ROWS, D_MODEL), F32)
    col = 0
    for f in f_list:
        f2d = f.reshape(-1, D_MODEL)
        buf = _moe_dispatch(f2d, lax.slice_in_dim(dest, col, col + f2d.shape[0], axis=1), buf)
        col += f2d.shape[0]
    y_sorted = _moe_ffn(buf, block_expert.astype(I32), n_used, w1, w3, w2)
    return y_sorted, dest


def kernel(x, c, ctx, c_ctx, ada_w, ada_b, norm_mix_w, norm_ffn_w, w_in, conv_a_w, conv_a_b, hgrn_lb,
           hgrn_norm_w, ssm_conv_w, ssm_conv_b, ssm_A_log, ssm_dt_bias, ssm_D, ssm_norm_w, w_branch, w_out,
           router_group_w, router_group_b, router_expert_w, router_expert_b, moe_w1, moe_w3, moe_w2,
           final_norm_w):
    depth = ada_w.shape[0]
    bsz, t_len, _ = x.shape
    t_ctx = ctx.shape[1]
    n_lat = bsz * t_len

    cond = jnp.concatenate([c, c_ctx[None, :], jnp.zeros((8 - bsz - 1, D_MODEL), F32)], axis=0)
    mods = _ada(cond, ada_w, ada_b)
    sc = {k: jnp.asarray(v) for k, v in _scan_constants().items()}
    fft_l = tuple(jnp.asarray(a) for a in _fft_constants(t_len))
    fft_c = tuple(jnp.asarray(a) for a in _fft_constants(t_ctx))
    e64 = sc["ebd"].astype(BF16)
    lane128 = np.arange(SLAB) // (2 * HEAD)
    e128 = jnp.asarray(lane128[:, None] == lane128[None, :], BF16)
    zero_state = (jnp.zeros((bsz, 2, SLAB, SLAB), F32), jnp.zeros((bsz, 2, 2 * HEAD, SLAB), F32))
    fnw = final_norm_w.reshape(1, D_MODEL)

    for l in range(depth):
        last = l == depth - 1
        six = mods[l].reshape(8, 6, D_MODEL)
        mod_l = jnp.pad(six[:bsz], ((0, 0), (0, 2), (0, 0)))
        mod_c = jnp.broadcast_to(jnp.pad(six[bsz], ((0, 2), (0, 0))), (bsz, 8, D_MODEL))

        wl = w_in[l]
        dt0 = 12 * SLAB
        w_dt = jnp.repeat(wl[:, dt0:dt0 + 2 * N_HEADS], HEAD, axis=1)
        w_ext = jnp.concatenate([wl[:, :dt0], w_dt], axis=1).astype(BF16)
        w_gate = wl[:, dt0 + 2 * N_HEADS:].astype(BF16)
        scw, scb = ssm_conv_w[l], ssm_conv_b[l][None, :]
        caw, cab = conv_a_w[l], conv_a_b[l][None, :]
        nmix, nffn = norm_mix_w[l][None, :], norm_ffn_w[l][None, :]
        alog_lane = jnp.repeat(ssm_A_log[l], HEAD, axis=1)
        dtbias_lane = jnp.repeat(ssm_dt_bias[l], HEAD, axis=1)
        hnw = jnp.tile(hgrn_norm_w[l], N_HEADS)[None, :]
        sd = jnp.repeat(ssm_D[l], HEAD)[None, :]
        snw = ssm_norm_w[l][None, :]
        wb, wo = w_branch[l].astype(BF16), w_out[l].astype(BF16)
        wr = jnp.concatenate([router_group_w[l].T, jnp.zeros((8 - N_GROUPS, D_MODEL), F32),
                              router_expert_w[l].T], axis=0)
        br = jnp.concatenate([router_group_b[l], jnp.zeros((8 - N_GROUPS,), F32),
                              router_expert_b[l]])[:, None]
        merge_w = (nmix, nffn, w_gate, wb, wo, hnw, sd, snw, e64, e128, wr, br)

        p3_c = _inproj(ctx, mod_c, nmix, w_ext, caw, cab, scw, scb)
        *scans_c, h_state, d_state = _scan(p3_c, hgrn_lb, alog_lane, dtbias_lane, zero_state, sc, l)
        p3_l = _inproj(x, mod_l, nmix, w_ext, caw, cab, scw, scb)
        *scans_l, _, _ = _scan(p3_l, hgrn_lb, alog_lane, dtbias_lane, (h_state, d_state), sc, l)
        yc_l = _fourier(p3_l, fft_l)
        x, f_l, ids, wts = _merge(x, p3_l, scans_l, yc_l, mod_l, *merge_w)
        f_list = [f_l]
        if not last:
            yc_c = _fourier(p3_c, fft_c)
            ctx, f_c, ids_c, wts_c = _merge(ctx, p3_c, scans_c, yc_c, mod_c, *merge_w)
            f_list.append(f_c)
            ids = jnp.concatenate([ids, ids_c], axis=1)
            wts = jnp.concatenate([wts, wts_c], axis=1)
        y_sorted, dest = _moe(f_list, ids, wts, moe_w1[l], moe_w3[l], moe_w2[l])
        x = _moe_combine(x, dest, wts, mod_l, fnw, y_sorted, 0, last)
        if not last:
            ctx = _moe_combine(ctx, dest, wts, mod_c, fnw, y_sorted, n_lat, False)
    return x
```

```python
import functools

import numpy as np
import jax
import jax.numpy as jnp
from jax import lax
from jax.experimental import pallas as pl
from jax.experimental.pallas import tpu as pltpu

F32, BF16, I32 = jnp.float32, jnp.bfloat16, jnp.int32
HIGHEST = lax.Precision.HIGHEST
EPS = 1e-6

D_MODEL = 1024
SLAB = 256
N_HEADS = 4
HEAD = 64
C_GDIM = 64
N_BRANCH = 4
CHUNK = 64
N_LEVELS = 6
N_GROUPS, EXPERTS_PER_GROUP = 4, 8
N_EXPERTS = N_GROUPS * EXPERTS_PER_GROUP
D_FF = 512
ROUTER_ROWS = 8 + N_EXPERTS
MOE_ROWS = 512
VMEM_LIMIT = 56 * 1024 * 1024

S_YA, S_Q, S_FF, S_FB, S_I, S_G, S_FOUR, S_Z, S_XS, S_BC, S_DTF, S_DTB = range(12)
N_SLABS = 12


def _sigmoid(x):
    return 1.0 / (1.0 + jnp.exp(-x))


def _softplus(x):
    return jnp.maximum(x, 0.0) + jnp.log1p(jnp.exp(-jnp.abs(x)))


def _tile(n, pref):
    t = pref
    while n % t:
        t //= 2
    return t


def _params(*sem):
    return pltpu.CompilerParams(dimension_semantics=sem, vmem_limit_bytes=VMEM_LIMIT)


def _nt(a, b, **kw):
    return lax.dot_general(a, b, (((1,), (1,)), ((), ())), preferred_element_type=F32, **kw)


def _tn(a, b, **kw):
    return lax.dot_general(a, b, (((0,), (0,)), ((), ())), preferred_element_type=F32, **kw)


def _ada_body(s_ref, w_ref, b_ref, o_ref):
    s = s_ref[...]
    s = s * _sigmoid(s)
    o_ref[...] = jnp.dot(s, w_ref[...], precision=HIGHEST, preferred_element_type=F32) + b_ref[...]


def _ada(cond, ada_w, ada_b):
    depth = ada_w.shape[0]
    n6 = ada_w.shape[2] // D_MODEL
    return pl.pallas_call(
        _ada_body,
        grid=(depth, n6),
        in_specs=[pl.BlockSpec((8, D_MODEL), lambda l, j: (0, 0)),
                  pl.BlockSpec((None, D_MODEL, D_MODEL), lambda l, j: (l, 0, j)),
                  pl.BlockSpec((None, 1, D_MODEL), lambda l, j: (l, 0, j))],
        out_specs=pl.BlockSpec((None, 8, D_MODEL), lambda l, j: (l, 0, j)),
        out_shape=jax.ShapeDtypeStruct((depth, 8, n6 * D_MODEL), F32),
        compiler_params=_params("parallel", "parallel"),
        name="ada_mod",
    )(cond, ada_w, ada_b.reshape(depth, 1, -1))


def _norm_mod(x, nw, shift, scale):
    ms = jnp.mean(x * x, axis=-1, keepdims=True)
    return (x * lax.rsqrt(ms + EPS) * nw) * (1.0 + scale) + shift


def _inproj_body(x_ref, xp_ref, xn_ref, mod_ref, nw_ref, w_ref, caw_ref, cab_ref, scw_ref, scb_ref,
                 p_ref, *, tm, nt):
    i = pl.program_id(1)
    xe = jnp.concatenate([xp_ref[...], x_ref[...], xn_ref[...]], axis=0)
    h = _norm_mod(xe, nw_ref[...], mod_ref[0:1, :], mod_ref[1:2, :])
    pr = jnp.dot(h.astype(BF16), w_ref[...], preferred_element_type=F32)
    rows = lax.broadcasted_iota(I32, (tm + 16, 1), 0)
    lo = jnp.where(i > 0, 0, 8)
    hi = jnp.where(i < nt - 1, tm + 16, tm + 8)
    vm = jnp.where((rows >= lo) & (rows < hi), 1.0, 0.0)

    def conv3(u, w, b):
        u = u * vm
        return u[7:tm + 7] * w[0:1, :] + u[8:tm + 8] * w[1:2, :] + u[9:tm + 9] * w[2:3, :] + b[...]

    ca = conv3(pr[:, SLAB:2 * SLAB] * pr[:, 2 * SLAB:3 * SLAB], caw_ref, cab_ref)
    p_ref[S_YA] = pr[8:tm + 8, 0:SLAB] * ca
    for s in range(7):
        p_ref[S_Q + s] = pr[8:tm + 8, (3 + s) * SLAB:(4 + s) * SLAB]
    cs = conv3(pr[:, 10 * SLAB:12 * SLAB], scw_ref, scb_ref)
    cs = cs * _sigmoid(cs)
    p_ref[S_XS] = cs[:, 0:SLAB]
    p_ref[S_BC] = cs[:, SLAB:2 * SLAB]
    p_ref[S_DTF] = pr[8:tm + 8, 12 * SLAB:13 * SLAB]
    p_ref[S_DTB] = pr[8:tm + 8, 13 * SLAB:14 * SLAB]


def _inproj(x, mod, nw, w_ext, caw, cab, scw, scb):
    bsz, t_len, _ = x.shape
    tm = min(512, t_len)
    nt = t_len // tm
    t8 = tm // 8
    full = lambda shape: pl.BlockSpec(shape, lambda b, i: (0,) * len(shape))
    return pl.pallas_call(
        functools.partial(_inproj_body, tm=tm, nt=nt),
        grid=(bsz, nt),
        in_specs=[pl.BlockSpec((None, tm, D_MODEL), lambda b, i: (b, i, 0)),
                  pl.BlockSpec((None, 8, D_MODEL), lambda b, i: (b, jnp.maximum(i * t8 - 1, 0), 0)),
                  pl.BlockSpec((None, 8, D_MODEL), lambda b, i: (b, jnp.minimum((i + 1) * t8, t_len // 8 - 1), 0)),
                  pl.BlockSpec((None, 8, D_MODEL), lambda b, i: (b, 0, 0)),
                  full((1, D_MODEL)), full(w_ext.shape), full(caw.shape), full(cab.shape),
                  full(scw.shape), full(scb.shape)],
        out_specs=pl.BlockSpec((N_SLABS, None, tm, SLAB), lambda b, i: (0, b, i, 0)),
        out_shape=jax.ShapeDtypeStruct((N_SLABS, bsz, t_len, SLAB), F32),
        compiler_params=_params("parallel", "parallel"),
        name="in_proj",
    )(x, x, x, mod, nw, w_ext, caw, cab, scw, scb)


def _scan_constants():
    t = np.arange(CHUNK)[:, None]
    r = np.arange(CHUNK)[None, :]
    out = {}
    for name, fwd in (("f", True), ("b", False)):
        blocks = [(r <= t) if fwd else (r >= t), (r > t) if fwd else (r < t)]
        qs, ks, masks = [], [], []
        for lvl in range(N_LEVELS):
            m = 1 << lvl
            blk = t // (2 * m)
            ref = blk * 2 * m + m - 1
            upper = (t % (2 * m)) >= m
            s_blk = (r // (2 * m))
            s_upper = (r % (2 * m)) >= m
            if fwd:
                qs.append(upper & (r > ref) & (r <= t))
                ks.append((~upper) & (r > t) & (r <= ref))
                masks.append(upper & (~s_upper) & (s_blk == blk))
            else:
                qs.append((~upper) & (r >= t) & (r <= ref))
                ks.append(upper & (r > ref) & (r < t))
                masks.append((~upper) & s_upper & (s_blk == blk))
        out["cm_" + name] = np.concatenate(blocks + [q | k for q, k in zip(qs, ks)], axis=0).astype(np.float32)
        out["lm_" + name] = np.stack([np.tile(mk, (1, N_HEADS)) for mk in masks]).astype(np.float32)
        out["su_" + name] = np.tile((r.T > r) if fwd else (r.T < r), (1, N_HEADS)).astype(np.float32)
        out["sc_" + name] = np.tile((t >= r) if fwd else (r >= t), (1, N_HEADS)).astype(np.float32)
    lane_head = np.arange(SLAB) // HEAD
    out["ebd"] = (lane_head[:, None] == lane_head[None, :]).astype(np.float32)
    lane_group = np.arange(2 * HEAD) // HEAD
    out["gm4"] = (lane_head[:, None] // 2 == lane_group[None, :]).astype(np.float32)
    return out


def _split_bf16(a):
    hi = a.astype(BF16)
    return hi, (a - hi.astype(F32)).astype(BF16)


def _head_blocks(a, mask):
    return jnp.concatenate([a.astype(BF16)] * N_HEADS, axis=0) * mask


def _hgrn_chunk(qv, kv, vv, logf, s_ref, k, cm, lm_ref, ebd, fwd):
    c = CHUNK
    e = jnp.exp(jnp.dot(cm, jnp.concatenate(_split_bf16(logf), axis=0), preferred_element_type=F32))
    eb = e.astype(BF16)
    qb, kb = qv.astype(BF16), kv.astype(BF16)
    p = jnp.zeros((c, SLAB), F32)
    for lvl in range(N_LEVELS):
        el = eb[(2 + lvl) * c:(3 + lvl) * c]
        p = p + _nt(qb * el, _head_blocks(kb * el, ebd)) * lm_ref[lvl]
    state = s_ref[k]
    o = jnp.dot((qv * kv).astype(BF16), ebd, preferred_element_type=F32) * vv
    o = o + jnp.dot(p.astype(BF16), _head_blocks(vv, ebd), preferred_element_type=F32)
    o = o + _nt((qv * e[0:c]).astype(BF16), state.astype(BF16))
    upd = _tn(vv.astype(BF16), (kv * e[c:2 * c]).astype(BF16))
    tot = e[c - 1:c] if fwd else e[0:1]
    s_ref[k] = state * tot + upd * ebd.astype(F32)
    return o


def _ssd_chunk(bc, xv, dt, a, s_ref, k, sm, um, caus, gm4, gms, ebd, fwd):
    c = CHUNK
    bv, cv = bc[:, :2 * HEAD], bc[:, 2 * HEAD:].astype(BF16)
    a_hi, a_lo = _split_bf16(a)
    ex = jnp.dot(sm, jnp.concatenate([a_hi, a_lo], axis=0), preferred_element_type=F32)
    cum, aft = ex[0:c], ex[c:]
    between = jnp.dot(sm[0:c], jnp.concatenate([a_hi * um, a_lo * um], axis=0),
                      preferred_element_type=F32)
    p = _nt(cv, _head_blocks(bv, gm4)) * (jnp.exp(between) * caus)
    xt = xv * dt
    state = s_ref[k]
    o = jnp.dot(p.astype(BF16), _head_blocks(xt, ebd), preferred_element_type=F32)
    o = o + jnp.dot(cv, state.astype(BF16), preferred_element_type=F32) * jnp.exp(cum)
    upd = _tn(bv.astype(BF16), (xt * jnp.exp(aft)).astype(BF16))
    tot = jnp.exp(cum[c - 1:c] if fwd else cum[0:1])
    s_ref[k] = state * tot + upd * gms
    return o


def _scan_body(qf, ff, vf, xsf, bcf, dtf, qb, fb, vb, xsb, bcb, dtb,
               lb_ref, alog_ref, dtbias_ref, cmf_ref, cmb_ref, lmf_ref, lmb_ref, suf_ref, sub_ref, scf_ref, scb_ref,
               gm4_ref, gms_ref, ebd_ref, h0_ref, d0_ref,
               ohf, ohb, osf, osb, hout_ref, dout_ref, sh_ref, sd_ref, *, layer, nck, nsteps):
    i = pl.program_id(1)

    @pl.when(i == 0)
    def _():
        sh_ref[...] = h0_ref[...]
        sd_ref[...] = d0_ref[...]

    ebd = ebd_ref[...]
    cmf, cmb = cmf_ref[...], cmb_ref[...]
    smf, smb = cmf_ref[0:2 * CHUNK, :], cmb_ref[0:2 * CHUNK, :]

    def lower_bound(d):
        rows = lb_ref[d]
        ex = jnp.exp(rows - jnp.max(rows, axis=0, keepdims=True))
        prob = ex / jnp.sum(ex, axis=0, keepdims=True)
        lb = jnp.zeros((1, SLAB), F32)
        for j in range(1, layer + 1):
            lb = lb + prob[j:j + 1]
        return lb

    lbs = [lower_bound(0), lower_bound(1)]

    def hgrn_inputs(q_raw, f_raw, lb):
        q = q_raw * _sigmoid(q_raw)
        log_sig = jnp.minimum(f_raw, 0.0) - jnp.log1p(jnp.exp(-jnp.abs(f_raw)))
        a = jnp.log(lb)
        b = jnp.log1p(-lb) + log_sig
        logf = jnp.maximum(a, b) + jnp.log1p(jnp.exp(-jnp.abs(a - b)))
        kk = (1.0 - lb) * _sigmoid(-f_raw)
        return q, kk, logf

    def ssd_inputs(dt_raw, d):
        dt = _softplus(dt_raw + dtbias_ref[d:d + 1, :])
        return dt, -jnp.exp(alog_ref[d:d + 1, :]) * dt

    def body(j, carry):
        rf = pl.ds(pl.multiple_of(j * CHUNK, CHUNK), CHUNK)
        rb = pl.ds(pl.multiple_of((nck - 1 - j) * CHUNK, CHUNK), CHUNK)
        q, kk, logf = hgrn_inputs(qf[rf, :], ff[rf, :], lbs[0])
        ohf[rf, :] = _hgrn_chunk(q, kk, vf[rf, :], logf, sh_ref, 0, cmf, lmf_ref, ebd, True)
        q, kk, logf = hgrn_inputs(qb[rb, :], fb[rb, :], lbs[1])
        ohb[rb, :] = _hgrn_chunk(q, kk, vb[rb, :], logf, sh_ref, 1, cmb, lmb_ref, ebd, False)
        dt, a = ssd_inputs(dtf[rf, :], 0)
        osf[rf, :] = _ssd_chunk(bcf[rf, :], xsf[rf, :], dt, a, sd_ref, 0, smf, suf_ref[...], scf_ref[...],
                                gm4_ref[...], gms_ref[...], ebd, True)
        dt, a = ssd_inputs(dtb[rb, :], 1)
        osb[rb, :] = _ssd_chunk(bcb[rb, :], xsb[rb, :], dt, a, sd_ref, 1, smb, sub_ref[...], scb_ref[...],
                                gm4_ref[...], gms_ref[...], ebd, False)
        return carry

    lax.fori_loop(0, nck, body, 0, unroll=2)

    @pl.when(i == nsteps - 1)
    def _():
        hout_ref[...] = sh_ref[...]
        dout_ref[...] = sd_ref[...]


def _scan(p3, hgrn_lb, alog_lane, dtbias_lane, s0, consts, layer):
    _, bsz, t_len, _ = p3.shape
    cb = min(256, t_len)
    nb = t_len // cb
    slab_f = lambda s: pl.BlockSpec((None, None, cb, SLAB), lambda b, i: (s, b, i, 0))
    slab_b = lambda s: pl.BlockSpec((None, None, cb, SLAB), lambda b, i: (s, b, nb - 1 - i, 0))
    full = lambda a: pl.BlockSpec(a.shape, lambda b, i: (0,) * a.ndim)
    bf = lambda name: consts[name].astype(BF16)
    cm2 = lambda name: jnp.concatenate([bf(name), bf(name)], axis=1)
    small = [hgrn_lb, alog_lane, dtbias_lane, cm2("cm_f"), cm2("cm_b"), consts["lm_f"], consts["lm_b"],
             bf("su_f"), bf("su_b"), consts["sc_f"], consts["sc_b"], bf("gm4"), consts["gm4"].T, bf("ebd")]
    o_f = pl.BlockSpec((None, cb, SLAB), lambda b, i: (b, i, 0))
    o_b = pl.BlockSpec((None, cb, SLAB), lambda b, i: (b, nb - 1 - i, 0))
    st_h = pl.BlockSpec((None, 2, SLAB, SLAB), lambda b, i: (b, 0, 0, 0))
    st_d = pl.BlockSpec((None, 2, 2 * HEAD, SLAB), lambda b, i: (b, 0, 0, 0))
    o_shape = jax.ShapeDtypeStruct((bsz, t_len, SLAB), F32)
    h0, d0 = s0
    return pl.pallas_call(
        functools.partial(_scan_body, layer=layer, nck=cb // CHUNK, nsteps=nb),
        grid=(bsz, nb),
        in_specs=([slab_f(s) for s in (S_Q, S_FF, S_I, S_XS, S_BC, S_DTF)]
                  + [slab_b(s) for s in (S_Q, S_FB, S_I, S_XS, S_BC, S_DTB)]
                  + [full(a) for a in small] + [st_h, st_d]),
        out_specs=[o_f, o_b, o_f, o_b, st_h, st_d],
        out_shape=[o_shape, o_shape, o_shape, o_shape, jax.ShapeDtypeStruct(h0.shape, F32),
                   jax.ShapeDtypeStruct(d0.shape, F32)],
        scratch_shapes=[pltpu.VMEM((2, SLAB, SLAB), F32), pltpu.VMEM((2, 2 * HEAD, SLAB), F32)],
        compiler_params=_params("parallel", "arbitrary"),
        name="chunk_scan",
    )(*([p3] * 12), *small, h0, d0)


def _fft_sizes(t_len):
    tf = 64 if t_len >= 4096 else 16
    return tf, t_len // tf


def _fft_constants(t_len):
    tf, ts = _fft_sizes(t_len)
    c = np.arange(SLAB)
    same = (c[:, None] // C_GDIM) == (c[None, :] // C_GDIM)
    ang = 2 * np.pi * ((c[:, None] % C_GDIM) * (c[None, :] % C_GDIM) % C_GDIM) / C_GDIM
    chan = np.concatenate([np.cos(ang) * same, -np.sin(ang) * same], axis=1)
    k2 = np.arange(ts)
    a = 2 * np.pi * (k2[:, None] * k2[None, :] % ts) / ts
    flat = np.block([[np.cos(a), np.sin(a)], [-np.sin(a), np.cos(a)]])
    k1 = np.arange(tf)[None, :, None]
    tfi = np.arange(tf)[None, None, :]
    kk2 = np.arange(ts)[:, None, None]
    th = 2 * np.pi * (((tfi * k1 * ts) + tfi * kk2) % t_len) / t_len
    scale = 1.0 / np.sqrt(t_len * C_GDIM)
    slab = np.concatenate([np.cos(th), np.sin(th)], axis=2) * scale
    return chan.astype(np.float32), flat.astype(np.float32), slab.astype(np.float32)


def _fft_flat_body(x_ref, chan_ref, flat_ref, y_ref, *, ts, groups):
    for g in range(groups):
        xg = x_ref[:, g * SLAB:(g + 1) * SLAB]
        a = jnp.dot(xg.astype(BF16), chan_ref[...], preferred_element_type=F32)
        z = jnp.concatenate([a[:, :SLAB], a[:, SLAB:]], axis=0).astype(BF16)
        y = jnp.dot(flat_ref[...], z, preferred_element_type=F32)
        y_ref[0, :, g * SLAB:(g + 1) * SLAB] = y[:ts]
        y_ref[1, :, g * SLAB:(g + 1) * SLAB] = y[ts:]


def _fft_slab_body(y_ref, m_ref, o_ref, *, kb):
    for j in range(kb):
        z = jnp.concatenate([y_ref[0, j], y_ref[1, j]], axis=0).astype(BF16)
        o_ref[j] = jnp.dot(m_ref[j], z, preferred_element_type=F32)


def _fourier(p3, consts):
    _, bsz, t_len, _ = p3.shape
    tf, ts = _fft_sizes(t_len)
    chan, flat, slab = (a.astype(BF16) for a in consts)
    groups = min(8, tf)
    lanes = groups * SLAB
    y = pl.pallas_call(
        functools.partial(_fft_flat_body, ts=ts, groups=groups),
        grid=(bsz, tf // groups),
        in_specs=[pl.BlockSpec((None, ts, lanes), lambda b, j: (b, 0, j)),
                  pl.BlockSpec(chan.shape, lambda b, j: (0, 0)),
                  pl.BlockSpec(flat.shape, lambda b, j: (0, 0))],
        out_specs=pl.BlockSpec((None, 2, ts, lanes), lambda b, j: (b, 0, 0, j)),
        out_shape=jax.ShapeDtypeStruct((bsz, 2, ts, tf * SLAB), F32),
        compiler_params=_params("parallel", "parallel"),
        name="fft_flat",
    )(p3[S_FOUR].reshape(bsz, ts, tf * SLAB), chan, flat)
    kb = min(8, ts)
    out = pl.pallas_call(
        functools.partial(_fft_slab_body, kb=kb),
        grid=(bsz, ts // kb),
        in_specs=[pl.BlockSpec((None, 2, kb, tf, SLAB), lambda b, j: (b, 0, j, 0, 0)),
                  pl.BlockSpec((kb, tf, 2 * tf), lambda b, j: (j, 0, 0))],
        out_specs=pl.BlockSpec((None, kb, tf, SLAB), lambda b, j: (b, j, 0, 0)),
        out_shape=jax.ShapeDtypeStruct((bsz, ts, tf, SLAB), F32),
        compiler_params=_params("parallel", "parallel"),
        name="fft_slab",
    )(y.reshape(bsz, 2, ts, tf, SLAB), slab)
    return out.transpose(0, 2, 1, 3).reshape(bsz, t_len, SLAB)


def _merge_body(x_ref, ya_ref, g_ref, z_ref, xs_ref, hf_ref, hb_ref, sf_ref, sb_ref, yc_ref,
                mod_ref, nmix_ref, nffn_ref, wg_ref, wb_ref, wo_ref, hnw_ref, sd_ref, snw_ref,
                e64_ref, e128_ref, wr_ref, br_ref,
                xo_ref, f_ref, ids_ref, wts_ref):
    x = x_ref[...]
    h = _norm_mod(x, nmix_ref[...], mod_ref[0:1, :], mod_ref[1:2, :]).astype(BF16)

    def group_rms(v, e_ref, width):
        ms = jnp.dot((v * v).astype(BF16), e_ref[...], preferred_element_type=F32) * (1.0 / width)
        return v * lax.rsqrt(ms + EPS)

    g = g_ref[...]
    y_b = group_rms(hf_ref[...] + hb_ref[...], e64_ref, HEAD) * hnw_ref[...] * (g * _sigmoid(g))
    z = z_ref[...]
    y_d = (sf_ref[...] + sb_ref[...] + sd_ref[...] * xs_ref[...]) * (z * _sigmoid(z))
    y_d = group_rms(y_d, e128_ref, 2 * HEAD) * snw_ref[...]
    branches = (ya_ref[...], y_b, yc_ref[...], y_d)
    merged = None
    for kb in range(N_BRANCH):
        gate = _sigmoid(jnp.dot(h, wg_ref[:, kb * D_MODEL:(kb + 1) * D_MODEL], preferred_element_type=F32))
        term = gate * jnp.dot(branches[kb].astype(BF16), wb_ref[kb], preferred_element_type=F32)
        merged = term if merged is None else merged + term
    mix = jnp.dot(merged.astype(BF16), wo_ref[...], preferred_element_type=F32)
    xo = x + mod_ref[2:3, :] * mix
    xo_ref[...] = xo
    f = _norm_mod(xo, nffn_ref[...], mod_ref[3:4, :], mod_ref[4:5, :])
    f_ref[...] = f

    lg = _nt(wr_ref[...], f, precision=HIGHEST) + br_ref[...]
    gl = lg[0:N_GROUPS]
    gidx = lax.broadcasted_iota(I32, gl.shape, 0)
    gmax = jnp.max(gl, axis=0, keepdims=True)
    g_top = jnp.min(jnp.where(gl == gmax, gidx, N_GROUPS), axis=0, keepdims=True)
    p_group = 1.0 / jnp.sum(jnp.exp(gl - gmax), axis=0, keepdims=True)
    e_in = jnp.zeros((EXPERTS_PER_GROUP, gl.shape[1]), F32)
    for grp in range(N_GROUPS):
        e_in = jnp.where(g_top == grp, lg[8 + 8 * grp:16 + 8 * grp], e_in)
    eidx = lax.broadcasted_iota(I32, e_in.shape, 0)
    v1 = jnp.max(e_in, axis=0, keepdims=True)
    i1 = jnp.min(jnp.where(e_in == v1, eidx, EXPERTS_PER_GROUP), axis=0, keepdims=True)
    rest = jnp.where(eidx == i1, -jnp.inf, e_in)
    v2 = jnp.max(rest, axis=0, keepdims=True)
    i2 = jnp.min(jnp.where(rest == v2, eidx, EXPERTS_PER_GROUP), axis=0, keepdims=True)
    w1 = 1.0 / (1.0 + jnp.exp(v2 - v1))
    row = lax.broadcasted_iota(I32, (8, gl.shape[1]), 0)
    base = g_top * EXPERTS_PER_GROUP
    ids_ref[...] = jnp.where(row == 0, base + i1, jnp.where(row == 1, base + i2, 0))
    wts_ref[...] = jnp.where(row == 0, p_group * w1, jnp.where(row == 1, p_group * (1.0 - w1), 0.0))


def _merge(x, p3, scans, y_c, mod, nmix, nffn, wg, wb, wo, hnw, sd, snw, e64, e128, wr, br):
    bsz, t_len, _ = x.shape
    tm = min(512, t_len)
    nt = t_len // tm
    row = lambda w: pl.BlockSpec((None, tm, w), lambda b, i: (b, i, 0))
    slab = lambda s: pl.BlockSpec((None, None, tm, SLAB), lambda b, i: (s, b, i, 0))
    full = lambda a: pl.BlockSpec(a.shape, lambda b, i: (0,) * a.ndim, pipeline_mode=pl.Buffered(1))
    consts = [nmix, nffn, wg, wb, wo, hnw, sd, snw, e64, e128, wr, br]
    lane_out = pl.BlockSpec((8, tm), lambda b, i: (0, b * nt + i))
    return pl.pallas_call(
        _merge_body,
        grid=(bsz, nt),
        in_specs=([row(D_MODEL)] + [slab(s) for s in (S_YA, S_G, S_Z, S_XS)] + [row(SLAB)] * 5
                  + [pl.BlockSpec((None, 8, D_MODEL), lambda b, i: (b, 0, 0))] + [full(a) for a in consts]),
        out_specs=[row(D_MODEL), row(D_MODEL), lane_out, lane_out],
        out_shape=[jax.ShapeDtypeStruct(x.shape, F32), jax.ShapeDtypeStruct(x.shape, F32),
                   jax.ShapeDtypeStruct((8, bsz * t_len), I32), jax.ShapeDtypeStruct((8, bsz * t_len), F32)],
        compiler_params=_params("parallel", "parallel"),
        name="merge_router",
    )(x, p3, p3, p3, p3, *scans, y_c, mod, *consts)


def _rank_body(ids_ref, tri_ref, rank_ref, cnt_ref, carry_ref, *, nsteps):
    i = pl.program_id(0)

    @pl.when(i == 0)
    def _():
        carry_ref[...] = jnp.zeros_like(carry_ref)

    tr = ids_ref.shape[1]
    eidx = lax.broadcasted_iota(I32, (N_EXPERTS, tr), 0)
    oh0 = jnp.where(eidx == ids_ref[0:1, :], 1.0, 0.0)
    oh1 = jnp.where(eidx == ids_ref[1:2, :], 1.0, 0.0)
    both = oh0 + oh1
    before = jnp.dot(both.astype(BF16), tri_ref[...], preferred_element_type=F32)
    pos = before + carry_ref[...]
    r0 = jnp.sum(oh0 * pos, axis=0, keepdims=True)
    r1 = jnp.sum(oh1 * pos, axis=0, keepdims=True)
    row = lax.broadcasted_iota(I32, (8, tr), 0)
    rank_ref[...] = jnp.where(row == 0, r0, jnp.where(row == 1, r1, 0.0)).astype(I32)
    carry_ref[...] = carry_ref[...] + jnp.sum(both, axis=1, keepdims=True)

    @pl.when(i == nsteps - 1)
    def _():
        cnt_ref[...] = carry_ref[...].astype(I32)


def _moe_rank(ids):
    n = ids.shape[1]
    tr = _tile(n, 1024)
    tri = jnp.triu(jnp.ones((tr, tr), BF16), k=1)
    return pl.pallas_call(
        functools.partial(_rank_body, nsteps=n // tr),
        grid=(n // tr,),
        in_specs=[pl.BlockSpec((8, tr), lambda i: (0, i)), pl.BlockSpec((tr, tr), lambda i: (0, 0))],
        out_specs=[pl.BlockSpec((8, tr), lambda i: (0, i)), pl.BlockSpec((N_EXPERTS, 1), lambda i: (0, 0))],
        out_shape=[jax.ShapeDtypeStruct((8, n), I32), jax.ShapeDtypeStruct((N_EXPERTS, 1), I32)],
        scratch_shapes=[pltpu.VMEM((N_EXPERTS, 1), F32)],
        compiler_params=_params("arbitrary"),
        name="moe_rank",
    )(ids, tri)


def _row_copy(src_hbm, dst_hbm, sem, src_row, dst_row):
    return pltpu.make_async_copy(src_hbm.at[pl.ds(src_row, 1)], dst_hbm.at[pl.ds(dst_row, 1)], sem)


def _dispatch_body(dest_ref, f_ref, buf_in, buf_hbm, sem, *, tg):
    del buf_in

    def start(r, c):
        _row_copy(f_ref, buf_hbm, sem, r, dest_ref[0, r]).start(priority=0)
        _row_copy(f_ref, buf_hbm, sem, r, dest_ref[1, r]).start(priority=1)
        return c

    lax.fori_loop(0, tg, start, 0, unroll=8)
    for _ in range(2):
        pltpu.make_async_copy(f_ref, buf_hbm.at[pl.ds(0, tg)], sem).wait()


def _moe_dispatch(f2d, dest, buf):
    n = f2d.shape[0]
    tg = _tile(n, 512)
    return pl.pallas_call(
        functools.partial(_dispatch_body, tg=tg),
        grid=(n // tg,),
        in_specs=[pl.BlockSpec((2, tg), lambda i: (0, i), memory_space=pltpu.SMEM),
                  pl.BlockSpec((tg, D_MODEL), lambda i: (i, 0)), pl.BlockSpec(memory_space=pl.ANY)],
        out_specs=pl.BlockSpec(memory_space=pl.ANY),
        out_shape=jax.ShapeDtypeStruct(buf.shape, buf.dtype),
        scratch_shapes=[pltpu.SemaphoreType.DMA],
        input_output_aliases={2: 0},
        compiler_params=_params("arbitrary"),
        name="moe_dispatch",
    )(dest, f2d, buf)


def _ffn_body(be_ref, nu_ref, x_ref, w1_ref, w3_ref, w2_ref, o_ref, w1b_ref, w3b_ref, w2b_ref):
    i = pl.program_id(0)
    used = i < nu_ref[0]
    fresh = (i == 0) | (be_ref[i] != be_ref[jnp.maximum(i - 1, 0)])

    @pl.when(used & fresh)
    def _():
        w1b_ref[...] = w1_ref[...].astype(BF16)
        w3b_ref[...] = w3_ref[...].astype(BF16)
        w2b_ref[...] = w2_ref[...].astype(BF16)

    @pl.when(used)
    def _():
        xb = x_ref[...].astype(BF16)
        a = jnp.dot(xb, w1b_ref[...], preferred_element_type=F32)
        b = jnp.dot(xb, w3b_ref[...], preferred_element_type=F32)
        hid = (a * _sigmoid(a) * b).astype(BF16)
        o_ref[...] = jnp.dot(hid, w2b_ref[...], preferred_element_type=F32)

    @pl.when(jnp.logical_not(used))
    def _():
        o_ref[...] = jnp.zeros_like(o_ref)


def _moe_ffn(buf, block_expert, n_used, w1, w3, w2, layer):
    nblk = buf.shape[0] // MOE_ROWS
    wspec = lambda shape: pl.BlockSpec((None, None) + shape, lambda i, be, nu: (layer, be[i], 0, 0))
    return pl.pallas_call(
        _ffn_body,
        grid_spec=pltpu.PrefetchScalarGridSpec(
            num_scalar_prefetch=2,
            grid=(nblk,),
            in_specs=[pl.BlockSpec((MOE_ROWS, D_MODEL), lambda i, be, nu: (i, 0)),
                      wspec((D_MODEL, D_FF)), wspec((D_MODEL, D_FF)), wspec((D_FF, D_MODEL))],
            out_specs=pl.BlockSpec((MOE_ROWS, D_MODEL), lambda i, be, nu: (i, 0)),
            scratch_shapes=[pltpu.VMEM((D_MODEL, D_FF), BF16), pltpu.VMEM((D_MODEL, D_FF), BF16),
                            pltpu.VMEM((D_FF, D_MODEL), BF16)]),
        out_shape=jax.ShapeDtypeStruct(buf.shape, F32),
        compiler_params=_params("arbitrary"),
        name="moe_ffn",
    )(block_expert, n_used, buf, w1, w3, w2)


def _combine_body(dest_ref, nxt_ref, x_ref, wts_ref, mod_ref, fnw_ref, y_hbm, o_ref, rows_ref, sems, *,
                  tc, nsteps, final):
    s = pl.program_id(0)
    slot = lax.rem(s, 2)

    def gather(idx_ref, slot_):
        def start(r, c):
            for k in range(2):
                pltpu.make_async_copy(y_hbm.at[pl.ds(idx_ref[k, r], 1)], rows_ref.at[slot_, k, pl.ds(r, 1)],
                                      sems.at[slot_]).start(priority=k)
            return c

        lax.fori_loop(0, tc, start, 0, unroll=8)

    @pl.when(s == 0)
    def _():
        gather(dest_ref, 0)

    @pl.when(s + 1 < nsteps)
    def _():
        gather(nxt_ref, 1 - slot)

    for k in range(2):
        pltpu.make_async_copy(y_hbm.at[pl.ds(0, tc)], rows_ref.at[slot, k], sems.at[slot]).wait()
    eye = jnp.where(lax.broadcasted_iota(I32, (tc, tc), 0) == lax.broadcasted_iota(I32, (tc, tc), 1), 1.0, 0.0)
    wcol = _nt(eye, wts_ref[...], precision=HIGHEST)
    y = wcol[:, 0:1] * rows_ref[slot, 0] + wcol[:, 1:2] * rows_ref[slot, 1]
    out = x_ref[...] + mod_ref[5:6, :] * y
    if final:
        ms = jnp.mean(out * out, axis=-1, keepdims=True)
        out = out * lax.rsqrt(ms + EPS) * fnw_ref[...]
    o_ref[...] = out


def _moe_combine(x, dest, wts, mod, fnw, y_sorted, col0, final):
    bsz, t_len, _ = x.shape
    tc = min(256, t_len)
    nt = t_len // tc
    c0 = col0 // tc
    nsteps = bsz * nt
    out = pl.pallas_call(
        functools.partial(_combine_body, tc=tc, nsteps=nsteps, final=final),
        grid=(nsteps,),
        in_specs=[pl.BlockSpec((2, tc), lambda s: (0, c0 + s), memory_space=pltpu.SMEM),
                  pl.BlockSpec((2, tc), lambda s: (0, c0 + jnp.minimum(s + 1, nsteps - 1)), memory_space=pltpu.SMEM),
                  pl.BlockSpec((tc, D_MODEL), lambda s: (s, 0)),
                  pl.BlockSpec((8, tc), lambda s: (0, c0 + s)),
                  pl.BlockSpec((None, 8, D_MODEL), lambda s: (s // nt, 0, 0)),
                  pl.BlockSpec((1, D_MODEL), lambda s: (0, 0)),
                  pl.BlockSpec(memory_space=pl.ANY)],
        out_specs=pl.BlockSpec((tc, D_MODEL), lambda s: (s, 0)),
        out_shape=jax.ShapeDtypeStruct((bsz * t_len, D_MODEL), F32),
        scratch_shapes=[pltpu.VMEM((2, 2, tc, D_MODEL), F32), pltpu.SemaphoreType.DMA((2,))],
        compiler_params=_params("arbitrary"),
        name="moe_combine",
    )(dest, dest, x.reshape(bsz * t_len, D_MODEL), wts, mod, fnw, y_sorted)
    return out.reshape(x.shape)


def _moe(f_list, ids, wts, w1, w3, w2, layer):
    n = ids.shape[1]
    rank, counts = _moe_rank(ids)
    counts = counts[:, 0]
    padded = (counts + MOE_ROWS - 1) // MOE_ROWS * MOE_ROWS
    pad_ends = jnp.cumsum(padded)
    pad_starts = pad_ends - padded
    onehot = ids[None, :2] == jnp.arange(N_EXPERTS, dtype=I32)[:, None, None]
    dest = jnp.sum(jnp.where(onehot, pad_starts[:, None, None], 0), axis=0) + rank[:2]
    nblk = (2 * n) // MOE_ROWS + N_EXPERTS
    block_start = jnp.arange(nblk, dtype=I32) * MOE_ROWS
    block_expert = jnp.minimum(jnp.sum(block_start[:, None] >= pad_ends[None, :], axis=-1), N_EXPERTS - 1)
    n_used = (pad_ends[-1:] // MOE_ROWS).astype(I32)
    buf = jnp.zeros((nblk * MOE_ROWS, D_MODEL), F32)
    col = 0
    for f in f_list:
        f2d = f.reshape(-1, D_MODEL)
        buf = _moe_dispatch(f2d, lax.slice_in_dim(dest, col, col + f2d.shape[0], axis=1), buf)
        col += f2d.shape[0]
    y_sorted = _moe_ffn(buf, block_expert.astype(I32), n_used, w1, w3, w2, layer)
    return y_sorted, dest


def kernel(x, c, ctx, c_ctx, ada_w, ada_b, norm_mix_w, norm_ffn_w, w_in, conv_a_w, conv_a_b, hgrn_lb,
           hgrn_norm_w, ssm_conv_w, ssm_conv_b, ssm_A_log, ssm_dt_bias, ssm_D, ssm_norm_w, w_branch, w_out,
           router_group_w, router_group_b, router_expert_w, router_expert_b, moe_w1, moe_w3, moe_w2,
           final_norm_w):
    depth = ada_w.shape[0]
    bsz, t_len, _ = x.shape
    t_ctx = ctx.shape[1]
    n_lat = bsz * t_len

    cond = jnp.concatenate([c, c_ctx[None, :], jnp.zeros((8 - bsz - 1, D_MODEL), F32)], axis=0)
    mods = _ada(cond, ada_w, ada_b)
    sc = {k: jnp.asarray(v) for k, v in _scan_constants().items()}
    fft_l = tuple(jnp.asarray(a) for a in _fft_constants(t_len))
    fft_c = tuple(jnp.asarray(a) for a in _fft_constants(t_ctx))
    e64 = sc["ebd"].astype(BF16)
    lane128 = np.arange(SLAB) // (2 * HEAD)
    e128 = jnp.asarray(lane128[:, None] == lane128[None, :], BF16)
    zero_state = (jnp.zeros((bsz, 2, SLAB, SLAB), F32), jnp.zeros((bsz, 2, 2 * HEAD, SLAB), F32))
    fnw = final_norm_w.reshape(1, D_MODEL)

    for l in range(depth):
        last = l == depth - 1
        six = mods[l].reshape(8, 6, D_MODEL)
        mod_l = jnp.pad(six[:bsz], ((0, 0), (0, 2), (0, 0)))
        mod_c = jnp.broadcast_to(jnp.pad(six[bsz], ((0, 2), (0, 0))), (bsz, 8, D_MODEL))

        wl = w_in[l]
        dt0 = 12 * SLAB
        w_dt = jnp.repeat(wl[:, dt0:dt0 + 2 * N_HEADS], HEAD, axis=1)
        w_ext = jnp.concatenate([wl[:, :dt0], w_dt], axis=1).astype(BF16)
        w_gate = wl[:, dt0 + 2 * N_HEADS:].astype(BF16)
        scw, scb = ssm_conv_w[l], ssm_conv_b[l][None, :]
        caw, cab = conv_a_w[l], conv_a_b[l][None, :]
        nmix, nffn = norm_mix_w[l][None, :], norm_ffn_w[l][None, :]
        alog_lane = jnp.repeat(ssm_A_log[l], HEAD, axis=1)
        dtbias_lane = jnp.repeat(ssm_dt_bias[l], HEAD, axis=1)
        hnw = jnp.tile(hgrn_norm_w[l], N_HEADS)[None, :]
        sd = jnp.repeat(ssm_D[l], HEAD)[None, :]
        snw = ssm_norm_w[l][None, :]
        wb, wo = w_branch[l].astype(BF16), w_out[l].astype(BF16)
        wr = jnp.concatenate([router_group_w[l].T, jnp.zeros((8 - N_GROUPS, D_MODEL), F32),
                              router_expert_w[l].T], axis=0)
        br = jnp.concatenate([router_group_b[l], jnp.zeros((8 - N_GROUPS,), F32),
                              router_expert_b[l]])[:, None]
        merge_w = (nmix, nffn, w_gate, wb, wo, hnw, sd, snw, e64, e128, wr, br)

        p3_c = _inproj(ctx, mod_c, nmix, w_ext, caw, cab, scw, scb)
        *scans_c, h_state, d_state = _scan(p3_c, hgrn_lb, alog_lane, dtbias_lane, zero_state, sc, l)
        p3_l = _inproj(x, mod_l, nmix, w_ext, caw, cab, scw, scb)
        *scans_l, _, _ = _scan(p3_l, hgrn_lb, alog_lane, dtbias_lane, (h_state, d_state), sc, l)
        yc_l = _fourier(p3_l, fft_l)
        x, f_l, ids, wts = _merge(x, p3_l, scans_l, yc_l, mod_l, *merge_w)
        f_list = [f_l]
        if not last:
            yc_c = _fourier(p3_c, fft_c)
            ctx, f_c, ids_c, wts_c = _merge(ctx, p3_c, scans_c, yc_c, mod_c, *merge_w)
            f_list.append(f_c)
            ids = jnp.concatenate([ids, ids_c], axis=1)
            wts = jnp.concatenate([wts, wts_c], axis=1)
        y_sorted, dest = _moe(f_list, ids, wts, moe_w1, moe_w3, moe_w2, l)
        x = _moe_combine(x, dest, wts, mod_l, fnw, y_sorted, 0, last)
        if not last:
            ctx = _moe_combine(ctx, dest, wts, mod_c, fnw, y_sorted, n_lat, False)
    return x
```

```python
import functools

import numpy as np
import jax
import jax.numpy as jnp
from jax import lax
from jax.experimental import pallas as pl
from jax.experimental.pallas import tpu as pltpu
from jax.experimental.pallas import tpu_sc as plsc

F32, BF16, I32 = jnp.float32, jnp.bfloat16, jnp.int32
HIGHEST = lax.Precision.HIGHEST
EPS = 1e-6

D_MODEL = 1024
SLAB = 256
N_HEADS = 4
HEAD = 64
C_GDIM = 64
N_BRANCH = 4
CHUNK = 64
N_LEVELS = 6
N_GROUPS, EXPERTS_PER_GROUP = 4, 8
N_EXPERTS = N_GROUPS * EXPERTS_PER_GROUP
D_FF = 512
ROUTER_ROWS = 8 + N_EXPERTS
MOE_ROWS = 512
VMEM_LIMIT = 56 * 1024 * 1024
SC_CORES, SC_SUBCORES = 2, 16
SC_WINDOW = 32

S_YA, S_Q, S_FF, S_FB, S_I, S_G, S_FOUR, S_Z, S_XS, S_BC, S_DTF, S_DTB = range(12)
N_SLABS = 12


def _sigmoid(x):
    return 1.0 / (1.0 + jnp.exp(-x))


def _softplus(x):
    return jnp.maximum(x, 0.0) + jnp.log1p(jnp.exp(-jnp.abs(x)))


def _tile(n, pref):
    t = pref
    while n % t:
        t //= 2
    return t


def _params(*sem):
    return pltpu.CompilerParams(dimension_semantics=sem, vmem_limit_bytes=VMEM_LIMIT)


def _nt(a, b, **kw):
    return lax.dot_general(a, b, (((1,), (1,)), ((), ())), preferred_element_type=F32, **kw)


def _tn(a, b, **kw):
    return lax.dot_general(a, b, (((0,), (0,)), ((), ())), preferred_element_type=F32, **kw)


def _ada_body(s_ref, w_ref, b_ref, o_ref):
    s = s_ref[...]
    s = s * _sigmoid(s)
    o_ref[...] = jnp.dot(s, w_ref[...], precision=HIGHEST, preferred_element_type=F32) + b_ref[...]


def _ada(cond, ada_w, ada_b):
    depth = ada_w.shape[0]
    n6 = ada_w.shape[2] // D_MODEL
    return pl.pallas_call(
        _ada_body,
        grid=(depth, n6),
        in_specs=[pl.BlockSpec((8, D_MODEL), lambda l, j: (0, 0)),
                  pl.BlockSpec((None, D_MODEL, D_MODEL), lambda l, j: (l, 0, j)),
                  pl.BlockSpec((None, 1, D_MODEL), lambda l, j: (l, 0, j))],
        out_specs=pl.BlockSpec((None, 8, D_MODEL), lambda l, j: (l, 0, j)),
        out_shape=jax.ShapeDtypeStruct((depth, 8, n6 * D_MODEL), F32),
        compiler_params=_params("parallel", "parallel"),
        name="ada_mod",
    )(cond, ada_w, ada_b.reshape(depth, 1, -1))


def _norm_mod(x, nw, shift, scale):
    ms = jnp.mean(x * x, axis=-1, keepdims=True)
    return (x * lax.rsqrt(ms + EPS) * nw) * (1.0 + scale) + shift


def _inproj_body(x_ref, xp_ref, xn_ref, mod_ref, nw_ref, w_ref, caw_ref, cab_ref, scw_ref, scb_ref,
                 p_ref, *, tm, nt):
    i = pl.program_id(1)
    xe = jnp.concatenate([xp_ref[...], x_ref[...], xn_ref[...]], axis=0)
    h = _norm_mod(xe, nw_ref[...], mod_ref[0:1, :], mod_ref[1:2, :])
    pr = jnp.dot(h.astype(BF16), w_ref[...], preferred_element_type=F32)
    rows = lax.broadcasted_iota(I32, (tm + 16, 1), 0)
    lo = jnp.where(i > 0, 0, 8)
    hi = jnp.where(i < nt - 1, tm + 16, tm + 8)
    vm = jnp.where((rows >= lo) & (rows < hi), 1.0, 0.0)

    def conv3(u, w, b):
        u = u * vm
        return u[7:tm + 7] * w[0:1, :] + u[8:tm + 8] * w[1:2, :] + u[9:tm + 9] * w[2:3, :] + b[...]

    ca = conv3(pr[:, SLAB:2 * SLAB] * pr[:, 2 * SLAB:3 * SLAB], caw_ref, cab_ref)
    p_ref[S_YA] = pr[8:tm + 8, 0:SLAB] * ca
    for s in range(7):
        p_ref[S_Q + s] = pr[8:tm + 8, (3 + s) * SLAB:(4 + s) * SLAB]
    cs = conv3(pr[:, 10 * SLAB:12 * SLAB], scw_ref, scb_ref)
    cs = cs * _sigmoid(cs)
    p_ref[S_XS] = cs[:, 0:SLAB]
    p_ref[S_BC] = cs[:, SLAB:2 * SLAB]
    p_ref[S_DTF] = pr[8:tm + 8, 12 * SLAB:13 * SLAB]
    p_ref[S_DTB] = pr[8:tm + 8, 13 * SLAB:14 * SLAB]


def _inproj(x, mod, nw, w_ext, caw, cab, scw, scb):
    bsz, t_len, _ = x.shape
    tm = min(512, t_len)
    nt = t_len // tm
    t8 = tm // 8
    full = lambda shape: pl.BlockSpec(shape, lambda b, i: (0,) * len(shape))
    return pl.pallas_call(
        functools.partial(_inproj_body, tm=tm, nt=nt),
        grid=(bsz, nt),
        in_specs=[pl.BlockSpec((None, tm, D_MODEL), lambda b, i: (b, i, 0)),
                  pl.BlockSpec((None, 8, D_MODEL), lambda b, i: (b, jnp.maximum(i * t8 - 1, 0), 0)),
                  pl.BlockSpec((None, 8, D_MODEL), lambda b, i: (b, jnp.minimum((i + 1) * t8, t_len // 8 - 1), 0)),
                  pl.BlockSpec((None, 8, D_MODEL), lambda b, i: (b, 0, 0)),
                  full((1, D_MODEL)), full(w_ext.shape), full(caw.shape), full(cab.shape),
                  full(scw.shape), full(scb.shape)],
        out_specs=pl.BlockSpec((N_SLABS, None, tm, SLAB), lambda b, i: (0, b, i, 0)),
        out_shape=jax.ShapeDtypeStruct((N_SLABS, bsz, t_len, SLAB), F32),
        compiler_params=_params("parallel", "parallel"),
        name="in_proj",
    )(x, x, x, mod, nw, w_ext, caw, cab, scw, scb)


def _scan_constants():
    t = np.arange(CHUNK)[:, None]
    r = np.arange(CHUNK)[None, :]
    out = {}
    for name, fwd in (("f", True), ("b", False)):
        blocks = [(r <= t) if fwd else (r >= t), (r > t) if fwd else (r < t)]
        qs, ks, masks = [], [], []
        for lvl in range(N_LEVELS):
            m = 1 << lvl
            blk = t // (2 * m)
            ref = blk * 2 * m + m - 1
            upper = (t % (2 * m)) >= m
            s_blk = (r // (2 * m))
            s_upper = (r % (2 * m)) >= m
            if fwd:
                qs.append(upper & (r > ref) & (r <= t))
                ks.append((~upper) & (r > t) & (r <= ref))
                masks.append(upper & (~s_upper) & (s_blk == blk))
            else:
                qs.append((~upper) & (r >= t) & (r <= ref))
                ks.append(upper & (r > ref) & (r < t))
                masks.append((~upper) & s_upper & (s_blk == blk))
        out["cm_" + name] = np.concatenate(blocks + [q | k for q, k in zip(qs, ks)], axis=0).astype(np.float32)
        out["lm_" + name] = np.stack([np.tile(mk, (1, N_HEADS)) for mk in masks]).astype(np.float32)
        out["su_" + name] = np.tile((r.T > r) if fwd else (r.T < r), (1, N_HEADS)).astype(np.float32)
        out["sc_" + name] = np.tile((t >= r) if fwd else (r >= t), (1, N_HEADS)).astype(np.float32)
    lane_head = np.arange(SLAB) // HEAD
    out["ebd"] = (lane_head[:, None] == lane_head[None, :]).astype(np.float32)
    lane_group = np.arange(2 * HEAD) // HEAD
    out["gm4"] = (lane_head[:, None] // 2 == lane_group[None, :]).astype(np.float32)
    return out


def _split_bf16(a):
    hi = a.astype(BF16)
    return hi, (a - hi.astype(F32)).astype(BF16)


def _head_blocks(a, mask):
    return jnp.concatenate([a.astype(BF16)] * N_HEADS, axis=0) * mask


def _hgrn_chunk(qv, kv, vv, logf, s_ref, k, cm, lm_ref, ebd, fwd):
    c = CHUNK
    e = jnp.exp(jnp.dot(cm, jnp.concatenate(_split_bf16(logf), axis=0), preferred_element_type=F32))
    eb = e.astype(BF16)
    qb, kb = qv.astype(BF16), kv.astype(BF16)
    p = jnp.zeros((c, SLAB), F32)
    for lvl in range(N_LEVELS):
        el = eb[(2 + lvl) * c:(3 + lvl) * c]
        p = p + _nt(qb * el, _head_blocks(kb * el, ebd)) * lm_ref[lvl]
    state = s_ref[k]
    o = jnp.dot((qv * kv).astype(BF16), ebd, preferred_element_type=F32) * vv
    o = o + jnp.dot(p.astype(BF16), _head_blocks(vv, ebd), preferred_element_type=F32)
    o = o + _nt((qv * e[0:c]).astype(BF16), state.astype(BF16))
    upd = _tn(vv.astype(BF16), (kv * e[c:2 * c]).astype(BF16))
    tot = e[c - 1:c] if fwd else e[0:1]
    s_ref[k] = state * tot + upd * ebd.astype(F32)
    return o


def _ssd_chunk(bc, xv, dt, a, s_ref, k, sm, um, caus, gm4, gms, ebd, fwd):
    c = CHUNK
    bv, cv = bc[:, :2 * HEAD], bc[:, 2 * HEAD:].astype(BF16)
    a_hi, a_lo = _split_bf16(a)
    ex = jnp.dot(sm, jnp.concatenate([a_hi, a_lo], axis=0), preferred_element_type=F32)
    cum, aft = ex[0:c], ex[c:]
    between = jnp.dot(sm[0:c], jnp.concatenate([a_hi * um, a_lo * um], axis=0),
                      preferred_element_type=F32)
    p = _nt(cv, _head_blocks(bv, gm4)) * (jnp.exp(between) * caus)
    xt = xv * dt
    state = s_ref[k]
    o = jnp.dot(p.astype(BF16), _head_blocks(xt, ebd), preferred_element_type=F32)
    o = o + jnp.dot(cv, state.astype(BF16), preferred_element_type=F32) * jnp.exp(cum)
    upd = _tn(bv.astype(BF16), (xt * jnp.exp(aft)).astype(BF16))
    tot = jnp.exp(cum[c - 1:c] if fwd else cum[0:1])
    s_ref[k] = state * tot + upd * gms
    return o


def _scan_body(qf, ff, vf, xsf, bcf, dtf, qb, fb, vb, xsb, bcb, dtb,
               lb_ref, alog_ref, dtbias_ref, cmf_ref, cmb_ref, lmf_ref, lmb_ref, suf_ref, sub_ref, scf_ref, scb_ref,
               gm4_ref, gms_ref, ebd_ref, h0_ref, d0_ref,
               ohf, ohb, osf, osb, hout_ref, dout_ref, sh_ref, sd_ref, *, layer, nck, nsteps):
    i = pl.program_id(1)

    @pl.when(i == 0)
    def _():
        sh_ref[...] = h0_ref[...]
        sd_ref[...] = d0_ref[...]

    ebd = ebd_ref[...]
    cmf, cmb = cmf_ref[...], cmb_ref[...]
    smf, smb = cmf_ref[0:2 * CHUNK, :], cmb_ref[0:2 * CHUNK, :]

    def lower_bound(d):
        rows = lb_ref[d]
        ex = jnp.exp(rows - jnp.max(rows, axis=0, keepdims=True))
        prob = ex / jnp.sum(ex, axis=0, keepdims=True)
        lb = jnp.zeros((1, SLAB), F32)
        for j in range(1, layer + 1):
            lb = lb + prob[j:j + 1]
        return lb

    lbs = [lower_bound(0), lower_bound(1)]

    def hgrn_inputs(q_raw, f_raw, lb):
        q = q_raw * _sigmoid(q_raw)
        log_sig = jnp.minimum(f_raw, 0.0) - jnp.log1p(jnp.exp(-jnp.abs(f_raw)))
        a = jnp.log(lb)
        b = jnp.log1p(-lb) + log_sig
        logf = jnp.maximum(a, b) + jnp.log1p(jnp.exp(-jnp.abs(a - b)))
        kk = (1.0 - lb) * _sigmoid(-f_raw)
        return q, kk, logf

    def ssd_inputs(dt_raw, d):
        dt = _softplus(dt_raw + dtbias_ref[d:d + 1, :])
        return dt, -jnp.exp(alog_ref[d:d + 1, :]) * dt

    def body(j, carry):
        rf = pl.ds(pl.multiple_of(j * CHUNK, CHUNK), CHUNK)
        rb = pl.ds(pl.multiple_of((nck - 1 - j) * CHUNK, CHUNK), CHUNK)
        q, kk, logf = hgrn_inputs(qf[rf, :], ff[rf, :], lbs[0])
        ohf[rf, :] = _hgrn_chunk(q, kk, vf[rf, :], logf, sh_ref, 0, cmf, lmf_ref, ebd, True)
        q, kk, logf = hgrn_inputs(qb[rb, :], fb[rb, :], lbs[1])
        ohb[rb, :] = _hgrn_chunk(q, kk, vb[rb, :], logf, sh_ref, 1, cmb, lmb_ref, ebd, False)
        dt, a = ssd_inputs(dtf[rf, :], 0)
        osf[rf, :] = _ssd_chunk(bcf[rf, :], xsf[rf, :], dt, a, sd_ref, 0, smf, suf_ref[...], scf_ref[...],
                                gm4_ref[...], gms_ref[...], ebd, True)
        dt, a = ssd_inputs(dtb[rb, :], 1)
        osb[rb, :] = _ssd_chunk(bcb[rb, :], xsb[rb, :], dt, a, sd_ref, 1, smb, sub_ref[...], scb_ref[...],
                                gm4_ref[...], gms_ref[...], ebd, False)
        return carry

    lax.fori_loop(0, nck, body, 0, unroll=2)

    @pl.when(i == nsteps - 1)
    def _():
        hout_ref[...] = sh_ref[...]
        dout_ref[...] = sd_ref[...]


def _scan(p3, hgrn_lb, alog_lane, dtbias_lane, s0, consts, layer):
    _, bsz, t_len, _ = p3.shape
    cb = min(256, t_len)
    nb = t_len // cb
    slab_f = lambda s: pl.BlockSpec((None, None, cb, SLAB), lambda b, i: (s, b, i, 0))
    slab_b = lambda s: pl.BlockSpec((None, None, cb, SLAB), lambda b, i: (s, b, nb - 1 - i, 0))
    full = lambda a: pl.BlockSpec(a.shape, lambda b, i: (0,) * a.ndim)
    bf = lambda name: consts[name].astype(BF16)
    cm2 = lambda name: jnp.concatenate([bf(name), bf(name)], axis=1)
    small = [hgrn_lb, alog_lane, dtbias_lane, cm2("cm_f"), cm2("cm_b"), consts["lm_f"], consts["lm_b"],
             bf("su_f"), bf("su_b"), consts["sc_f"], consts["sc_b"], bf("gm4"), consts["gm4"].T, bf("ebd")]
    o_f = pl.BlockSpec((None, cb, SLAB), lambda b, i: (b, i, 0))
    o_b = pl.BlockSpec((None, cb, SLAB), lambda b, i: (b, nb - 1 - i, 0))
    st_h = pl.BlockSpec((None, 2, SLAB, SLAB), lambda b, i: (b, 0, 0, 0))
    st_d = pl.BlockSpec((None, 2, 2 * HEAD, SLAB), lambda b, i: (b, 0, 0, 0))
    o_shape = jax.ShapeDtypeStruct((bsz, t_len, SLAB), F32)
    h0, d0 = s0
    return pl.pallas_call(
        functools.partial(_scan_body, layer=layer, nck=cb // CHUNK, nsteps=nb),
        grid=(bsz, nb),
        in_specs=([slab_f(s) for s in (S_Q, S_FF, S_I, S_XS, S_BC, S_DTF)]
                  + [slab_b(s) for s in (S_Q, S_FB, S_I, S_XS, S_BC, S_DTB)]
                  + [full(a) for a in small] + [st_h, st_d]),
        out_specs=[o_f, o_b, o_f, o_b, st_h, st_d],
        out_shape=[o_shape, o_shape, o_shape, o_shape, jax.ShapeDtypeStruct(h0.shape, F32),
                   jax.ShapeDtypeStruct(d0.shape, F32)],
        scratch_shapes=[pltpu.VMEM((2, SLAB, SLAB), F32), pltpu.VMEM((2, 2 * HEAD, SLAB), F32)],
        compiler_params=_params("parallel", "arbitrary"),
        name="chunk_scan",
    )(*([p3] * 12), *small, h0, d0)


def _fft_sizes(t_len):
    tf = 64 if t_len >= 4096 else 16
    return tf, t_len // tf


def _fft_constants(t_len):
    tf, ts = _fft_sizes(t_len)
    c = np.arange(SLAB)
    same = (c[:, None] // C_GDIM) == (c[None, :] // C_GDIM)
    ang = 2 * np.pi * ((c[:, None] % C_GDIM) * (c[None, :] % C_GDIM) % C_GDIM) / C_GDIM
    chan = np.concatenate([np.cos(ang) * same, -np.sin(ang) * same], axis=1)
    k2 = np.arange(ts)
    a = 2 * np.pi * (k2[:, None] * k2[None, :] % ts) / ts
    flat = np.block([[np.cos(a), np.sin(a)], [-np.sin(a), np.cos(a)]])
    k1 = np.arange(tf)[None, :, None]
    tfi = np.arange(tf)[None, None, :]
    kk2 = np.arange(ts)[:, None, None]
    th = 2 * np.pi * (((tfi * k1 * ts) + tfi * kk2) % t_len) / t_len
    scale = 1.0 / np.sqrt(t_len * C_GDIM)
    slab = np.concatenate([np.cos(th), np.sin(th)], axis=2) * scale
    return chan.astype(np.float32), flat.astype(np.float32), slab.astype(np.float32)


def _fft_flat_body(x_ref, chan_ref, flat_ref, y_ref, *, ts, groups):
    for g in range(groups):
        xg = x_ref[:, g * SLAB:(g + 1) * SLAB]
        a = jnp.dot(xg.astype(BF16), chan_ref[...], preferred_element_type=F32)
        z = jnp.concatenate([a[:, :SLAB], a[:, SLAB:]], axis=0).astype(BF16)
        y = jnp.dot(flat_ref[...], z, preferred_element_type=F32)
        y_ref[0, :, g * SLAB:(g + 1) * SLAB] = y[:ts]
        y_ref[1, :, g * SLAB:(g + 1) * SLAB] = y[ts:]


def _fft_slab_body(y_ref, m_ref, o_ref, *, kb):
    for j in range(kb):
        z = jnp.concatenate([y_ref[0, j], y_ref[1, j]], axis=0).astype(BF16)
        o_ref[j] = jnp.dot(m_ref[j], z, preferred_element_type=F32)


def _fourier(p3, consts):
    _, bsz, t_len, _ = p3.shape
    tf, ts = _fft_sizes(t_len)
    chan, flat, slab = (a.astype(BF16) for a in consts)
    groups = min(8, tf)
    lanes = groups * SLAB
    y = pl.pallas_call(
        functools.partial(_fft_flat_body, ts=ts, groups=groups),
        grid=(bsz, tf // groups),
        in_specs=[pl.BlockSpec((None, ts, lanes), lambda b, j: (b, 0, j)),
                  pl.BlockSpec(chan.shape, lambda b, j: (0, 0)),
                  pl.BlockSpec(flat.shape, lambda b, j: (0, 0))],
        out_specs=pl.BlockSpec((None, 2, ts, lanes), lambda b, j: (b, 0, 0, j)),
        out_shape=jax.ShapeDtypeStruct((bsz, 2, ts, tf * SLAB), F32),
        compiler_params=_params("parallel", "parallel"),
        name="fft_flat",
    )(p3[S_FOUR].reshape(bsz, ts, tf * SLAB), chan, flat)
    kb = min(8, ts)
    out = pl.pallas_call(
        functools.partial(_fft_slab_body, kb=kb),
        grid=(bsz, ts // kb),
        in_specs=[pl.BlockSpec((None, 2, kb, tf, SLAB), lambda b, j: (b, 0, j, 0, 0)),
                  pl.BlockSpec((kb, tf, 2 * tf), lambda b, j: (j, 0, 0))],
        out_specs=pl.BlockSpec((None, kb, tf, SLAB), lambda b, j: (b, j, 0, 0)),
        out_shape=jax.ShapeDtypeStruct((bsz, ts, tf, SLAB), F32),
        compiler_params=_params("parallel", "parallel"),
        name="fft_slab",
    )(y.reshape(bsz, 2, ts, tf, SLAB), slab)
    return out.transpose(0, 2, 1, 3).reshape(bsz, t_len, SLAB)


def _merge_body(x_ref, ya_ref, g_ref, z_ref, xs_ref, hf_ref, hb_ref, sf_ref, sb_ref, yc_ref,
                mod_ref, nmix_ref, nffn_ref, wg_ref, wb_ref, wo_ref, hnw_ref, sd_ref, snw_ref,
                e64_ref, e128_ref, wr_ref, br_ref,
                xo_ref, f_ref, ids_ref, wts_ref):
    x = x_ref[...]
    h = _norm_mod(x, nmix_ref[...], mod_ref[0:1, :], mod_ref[1:2, :]).astype(BF16)

    def group_rms(v, e_ref, width):
        ms = jnp.dot((v * v).astype(BF16), e_ref[...], preferred_element_type=F32) * (1.0 / width)
        return v * lax.rsqrt(ms + EPS)

    g = g_ref[...]
    y_b = group_rms(hf_ref[...] + hb_ref[...], e64_ref, HEAD) * hnw_ref[...] * (g * _sigmoid(g))
    z = z_ref[...]
    y_d = (sf_ref[...] + sb_ref[...] + sd_ref[...] * xs_ref[...]) * (z * _sigmoid(z))
    y_d = group_rms(y_d, e128_ref, 2 * HEAD) * snw_ref[...]
    branches = (ya_ref[...], y_b, yc_ref[...], y_d)
    merged = None
    for kb in range(N_BRANCH):
        gate = _sigmoid(jnp.dot(h, wg_ref[:, kb * D_MODEL:(kb + 1) * D_MODEL], preferred_element_type=F32))
        term = gate * jnp.dot(branches[kb].astype(BF16), wb_ref[kb], preferred_element_type=F32)
        merged = term if merged is None else merged + term
    mix = jnp.dot(merged.astype(BF16), wo_ref[...], preferred_element_type=F32)
    xo = x + mod_ref[2:3, :] * mix
    xo_ref[...] = xo
    f = _norm_mod(xo, nffn_ref[...], mod_ref[3:4, :], mod_ref[4:5, :])
    f_ref[...] = f

    lg = _nt(wr_ref[...], f, precision=HIGHEST) + br_ref[...]
    gl = lg[0:N_GROUPS]
    gidx = lax.broadcasted_iota(I32, gl.shape, 0)
    gmax = jnp.max(gl, axis=0, keepdims=True)
    g_top = jnp.min(jnp.where(gl == gmax, gidx, N_GROUPS), axis=0, keepdims=True)
    p_group = 1.0 / jnp.sum(jnp.exp(gl - gmax), axis=0, keepdims=True)
    e_in = jnp.zeros((EXPERTS_PER_GROUP, gl.shape[1]), F32)
    for grp in range(N_GROUPS):
        e_in = jnp.where(g_top == grp, lg[8 + 8 * grp:16 + 8 * grp], e_in)
    eidx = lax.broadcasted_iota(I32, e_in.shape, 0)
    v1 = jnp.max(e_in, axis=0, keepdims=True)
    i1 = jnp.min(jnp.where(e_in == v1, eidx, EXPERTS_PER_GROUP), axis=0, keepdims=True)
    rest = jnp.where(eidx == i1, -jnp.inf, e_in)
    v2 = jnp.max(rest, axis=0, keepdims=True)
    i2 = jnp.min(jnp.where(rest == v2, eidx, EXPERTS_PER_GROUP), axis=0, keepdims=True)
    w1 = 1.0 / (1.0 + jnp.exp(v2 - v1))
    row = lax.broadcasted_iota(I32, (8, gl.shape[1]), 0)
    base = g_top * EXPERTS_PER_GROUP
    ids_ref[...] = jnp.where(row == 0, base + i1, jnp.where(row == 1, base + i2, 0))
    wts_ref[...] = jnp.where(row == 0, p_group * w1, jnp.where(row == 1, p_group * (1.0 - w1), 0.0))


def _merge(x, p3, scans, y_c, mod, nmix, nffn, wg, wb, wo, hnw, sd, snw, e64, e128, wr, br):
    bsz, t_len, _ = x.shape
    tm = min(512, t_len)
    nt = t_len // tm
    row = lambda w: pl.BlockSpec((None, tm, w), lambda b, i: (b, i, 0))
    slab = lambda s: pl.BlockSpec((None, None, tm, SLAB), lambda b, i: (s, b, i, 0))
    full = lambda a: pl.BlockSpec(a.shape, lambda b, i: (0,) * a.ndim, pipeline_mode=pl.Buffered(1))
    consts = [nmix, nffn, wg, wb, wo, hnw, sd, snw, e64, e128, wr, br]
    lane_out = pl.BlockSpec((8, tm), lambda b, i: (0, b * nt + i))
    return pl.pallas_call(
        _merge_body,
        grid=(bsz, nt),
        in_specs=([row(D_MODEL)] + [slab(s) for s in (S_YA, S_G, S_Z, S_XS)] + [row(SLAB)] * 5
                  + [pl.BlockSpec((None, 8, D_MODEL), lambda b, i: (b, 0, 0))] + [full(a) for a in consts]),
        out_specs=[row(D_MODEL), row(D_MODEL), lane_out, lane_out],
        out_shape=[jax.ShapeDtypeStruct(x.shape, F32), jax.ShapeDtypeStruct(x.shape, F32),
                   jax.ShapeDtypeStruct((8, bsz * t_len), I32), jax.ShapeDtypeStruct((8, bsz * t_len), F32)],
        compiler_params=_params("parallel", "parallel"),
        name="merge_router",
    )(x, p3, p3, p3, p3, *scans, y_c, mod, *consts)


def _rank_body(ids_ref, tri_ref, rank_ref, cnt_ref, carry_ref, *, nsteps):
    i = pl.program_id(0)

    @pl.when(i == 0)
    def _():
        carry_ref[...] = jnp.zeros_like(carry_ref)

    tr = ids_ref.shape[1]
    eidx = lax.broadcasted_iota(I32, (N_EXPERTS, tr), 0)
    oh0 = jnp.where(eidx == ids_ref[0:1, :], 1.0, 0.0)
    oh1 = jnp.where(eidx == ids_ref[1:2, :], 1.0, 0.0)
    both = oh0 + oh1
    before = jnp.dot(both.astype(BF16), tri_ref[...], preferred_element_type=F32)
    pos = before + carry_ref[...]
    r0 = jnp.sum(oh0 * pos, axis=0, keepdims=True)
    r1 = jnp.sum(oh1 * pos, axis=0, keepdims=True)
    row = lax.broadcasted_iota(I32, (8, tr), 0)
    rank_ref[...] = jnp.where(row == 0, r0, jnp.where(row == 1, r1, 0.0)).astype(I32)
    carry_ref[...] = carry_ref[...] + jnp.sum(both, axis=1, keepdims=True)

    @pl.when(i == nsteps - 1)
    def _():
        cnt_ref[...] = carry_ref[...].astype(I32)


def _moe_rank(ids):
    n = ids.shape[1]
    tr = _tile(n, 1024)
    tri = jnp.triu(jnp.ones((tr, tr), BF16), k=1)
    return pl.pallas_call(
        functools.partial(_rank_body, nsteps=n // tr),
        grid=(n // tr,),
        in_specs=[pl.BlockSpec((8, tr), lambda i: (0, i)), pl.BlockSpec((tr, tr), lambda i: (0, 0))],
        out_specs=[pl.BlockSpec((8, tr), lambda i: (0, i)), pl.BlockSpec((N_EXPERTS, 1), lambda i: (0, 0))],
        out_shape=[jax.ShapeDtypeStruct((8, n), I32), jax.ShapeDtypeStruct((N_EXPERTS, 1), I32)],
        scratch_shapes=[pltpu.VMEM((N_EXPERTS, 1), F32)],
        compiler_params=_params("arbitrary"),
        name="moe_rank",
    )(ids, tri)


def _row_copy(src_hbm, dst_hbm, sem, src_row, dst_row):
    return pltpu.make_async_copy(src_hbm.at[pl.ds(src_row, 1)], dst_hbm.at[pl.ds(dst_row, 1)], sem)


def _dispatch_body(dest_ref, f_ref, buf_in, buf_hbm, sem, *, tg):
    del buf_in

    def start(r, c):
        _row_copy(f_ref, buf_hbm, sem, r, dest_ref[0, r]).start(priority=0)
        _row_copy(f_ref, buf_hbm, sem, r, dest_ref[1, r]).start(priority=1)
        return c

    lax.fori_loop(0, tg, start, 0, unroll=8)
    for _ in range(2):
        pltpu.make_async_copy(f_ref, buf_hbm.at[pl.ds(0, tg)], sem).wait()


def _moe_dispatch(f2d, dest, buf):
    n = f2d.shape[0]
    tg = _tile(n, 512)
    return pl.pallas_call(
        functools.partial(_dispatch_body, tg=tg),
        grid=(n // tg,),
        in_specs=[pl.BlockSpec((2, tg), lambda i: (0, i), memory_space=pltpu.SMEM),
                  pl.BlockSpec((tg, D_MODEL), lambda i: (i, 0)), pl.BlockSpec(memory_space=pl.ANY)],
        out_specs=pl.BlockSpec(memory_space=pl.ANY),
        out_shape=jax.ShapeDtypeStruct(buf.shape, buf.dtype),
        scratch_shapes=[pltpu.SemaphoreType.DMA],
        input_output_aliases={2: 0},
        compiler_params=_params("arbitrary"),
        name="moe_dispatch",
    )(dest, f2d, buf)


def _ffn_body(be_ref, nu_ref, x_ref, w1_ref, w3_ref, w2_ref, o_ref, w1b_ref, w3b_ref, w2b_ref):
    i = pl.program_id(0)
    used = i < nu_ref[0]
    fresh = (i == 0) | (be_ref[i] != be_ref[jnp.maximum(i - 1, 0)])

    @pl.when(used & fresh)
    def _():
        w1b_ref[...] = w1_ref[...].astype(BF16)
        w3b_ref[...] = w3_ref[...].astype(BF16)
        w2b_ref[...] = w2_ref[...].astype(BF16)

    @pl.when(used)
    def _():
        xb = x_ref[...].astype(BF16)
        a = jnp.dot(xb, w1b_ref[...], preferred_element_type=F32)
        b = jnp.dot(xb, w3b_ref[...], preferred_element_type=F32)
        hid = (a * _sigmoid(a) * b).astype(BF16)
        o_ref[...] = jnp.dot(hid, w2b_ref[...], preferred_element_type=F32)

    @pl.when(jnp.logical_not(used))
    def _():
        o_ref[...] = jnp.zeros_like(o_ref)


def _moe_ffn(buf, block_expert, n_used, w1, w3, w2, layer):
    nblk = buf.shape[0] // MOE_ROWS
    wspec = lambda shape: pl.BlockSpec((None, None) + shape, lambda i, be, nu: (layer, be[i], 0, 0))
    return pl.pallas_call(
        _ffn_body,
        grid_spec=pltpu.PrefetchScalarGridSpec(
            num_scalar_prefetch=2,
            grid=(nblk,),
            in_specs=[pl.BlockSpec((MOE_ROWS, D_MODEL), lambda i, be, nu: (i, 0)),
                      wspec((D_MODEL, D_FF)), wspec((D_MODEL, D_FF)), wspec((D_FF, D_MODEL))],
            out_specs=pl.BlockSpec((MOE_ROWS, D_MODEL), lambda i, be, nu: (i, 0)),
            scratch_shapes=[pltpu.VMEM((D_MODEL, D_FF), BF16), pltpu.VMEM((D_MODEL, D_FF), BF16),
                            pltpu.VMEM((D_FF, D_MODEL), BF16)]),
        out_shape=jax.ShapeDtypeStruct(buf.shape, F32),
        compiler_params=_params("arbitrary"),
        name="moe_ffn",
    )(block_expert, n_used, buf, w1, w3, w2)


def _combine_body(dest_ref, nxt_ref, x_ref, wts_ref, mod_ref, fnw_ref, y_hbm, o_ref, rows_ref, sems, *,
                  tc, nsteps, final):
    s = pl.program_id(0)
    slot = lax.rem(s, 2)

    def gather(idx_ref, slot_):
        def start(r, c):
            for k in range(2):
                pltpu.make_async_copy(y_hbm.at[pl.ds(idx_ref[k, r], 1)], rows_ref.at[slot_, k, pl.ds(r, 1)],
                                      sems.at[slot_]).start(priority=k)
            return c

        lax.fori_loop(0, tc, start, 0, unroll=8)

    @pl.when(s == 0)
    def _():
        gather(dest_ref, 0)

    @pl.when(s + 1 < nsteps)
    def _():
        gather(nxt_ref, 1 - slot)

    for k in range(2):
        pltpu.make_async_copy(y_hbm.at[pl.ds(0, tc)], rows_ref.at[slot, k], sems.at[slot]).wait()
    eye = jnp.where(lax.broadcasted_iota(I32, (tc, tc), 0) == lax.broadcasted_iota(I32, (tc, tc), 1), 1.0, 0.0)
    wcol = _nt(eye, wts_ref[...], precision=HIGHEST)
    y = wcol[:, 0:1] * rows_ref[slot, 0] + wcol[:, 1:2] * rows_ref[slot, 1]
    out = x_ref[...] + mod_ref[5:6, :] * y
    if final:
        ms = jnp.mean(out * out, axis=-1, keepdims=True)
        out = out * lax.rsqrt(ms + EPS) * fnw_ref[...]
    o_ref[...] = out


def _moe_combine(x, dest, wts, mod, fnw, y_sorted, col0, final):
    bsz, t_len, _ = x.shape
    tc = min(256, t_len)
    nt = t_len // tc
    c0 = col0 // tc
    nsteps = bsz * nt
    out = pl.pallas_call(
        functools.partial(_combine_body, tc=tc, nsteps=nsteps, final=final),
        grid=(nsteps,),
        in_specs=[pl.BlockSpec((2, tc), lambda s: (0, c0 + s), memory_space=pltpu.SMEM),
                  pl.BlockSpec((2, tc), lambda s: (0, c0 + jnp.minimum(s + 1, nsteps - 1)), memory_space=pltpu.SMEM),
                  pl.BlockSpec((tc, D_MODEL), lambda s: (s, 0)),
                  pl.BlockSpec((8, tc), lambda s: (0, c0 + s)),
                  pl.BlockSpec((None, 8, D_MODEL), lambda s: (s // nt, 0, 0)),
                  pl.BlockSpec((1, D_MODEL), lambda s: (0, 0)),
                  pl.BlockSpec(memory_space=pl.ANY)],
        out_specs=pl.BlockSpec((tc, D_MODEL), lambda s: (s, 0)),
        out_shape=jax.ShapeDtypeStruct((bsz * t_len, D_MODEL), F32),
        scratch_shapes=[pltpu.VMEM((2, 2, tc, D_MODEL), F32), pltpu.SemaphoreType.DMA((2,))],
        compiler_params=_params("arbitrary"),
        name="moe_combine",
    )(dest, dest, x.reshape(bsz * t_len, D_MODEL), wts, mod, fnw, y_sorted)
    return out.reshape(x.shape)


def _sc_gather_rows(src, idx):
    m, width = idx.shape[0], src.shape[1]
    mesh = plsc.VectorSubcoreMesh(core_axis_name="c", subcore_axis_name="s",
                                  num_cores=SC_CORES, num_subcores=SC_SUBCORES)

    @functools.partial(pl.kernel, out_type=jax.ShapeDtypeStruct((m, width), src.dtype), mesh=mesh,
                       scratch_types=[], name="sc_gather_rows",
                       compiler_params=pltpu.CompilerParams(use_tc_tiling_on_sc=True))
    def gather(src_hbm, idx_hbm, out_hbm):
        def body(idx_vmem, out_vmem):
            pltpu.sync_copy(src_hbm.at[idx_vmem.at[0]], out_vmem)

        pltpu.emit_pipeline(
            body,
            grid=(m // SC_WINDOW,),
            in_specs=[pl.BlockSpec((None, 1, SC_WINDOW), lambda i: (i, 0, 0))],
            out_specs=[pl.BlockSpec((SC_WINDOW, width), lambda i: (i, 0))],
            core_axis_name=("c", "s"),
            dimension_semantics=(pltpu.PARALLEL,),
        )(idx_hbm, out_hbm)

    return gather(src, idx.reshape(m // SC_WINDOW, 1, SC_WINDOW))


def _combine_dense_body(x_ref, g0_ref, g1_ref, wts_ref, mod_ref, fnw_ref, o_ref, *, tc, final):
    eye = jnp.where(lax.broadcasted_iota(I32, (tc, tc), 0) == lax.broadcasted_iota(I32, (tc, tc), 1), 1.0, 0.0)
    wcol = _nt(eye, wts_ref[...], precision=HIGHEST)
    y = wcol[:, 0:1] * g0_ref[...] + wcol[:, 1:2] * g1_ref[...]
    out = x_ref[...] + mod_ref[5:6, :] * y
    if final:
        ms = jnp.mean(out * out, axis=-1, keepdims=True)
        out = out * lax.rsqrt(ms + EPS) * fnw_ref[...]
    o_ref[...] = out


def _combine_dense(x, gathered, wts, mod, fnw, col0, final):
    bsz, t_len, _ = x.shape
    tc = min(256, t_len)
    nt = t_len // tc
    c0 = col0 // tc
    g_spec = lambda k: pl.BlockSpec((None, tc, D_MODEL), lambda b, i: (k, c0 + b * nt + i, 0))
    return pl.pallas_call(
        functools.partial(_combine_dense_body, tc=tc, final=final),
        grid=(bsz, nt),
        in_specs=[pl.BlockSpec((None, tc, D_MODEL), lambda b, i: (b, i, 0)), g_spec(0), g_spec(1),
                  pl.BlockSpec((8, tc), lambda b, i: (0, c0 + b * nt + i)),
                  pl.BlockSpec((None, 8, D_MODEL), lambda b, i: (b, 0, 0)),
                  pl.BlockSpec((1, D_MODEL), lambda b, i: (0, 0))],
        out_specs=pl.BlockSpec((None, tc, D_MODEL), lambda b, i: (b, i, 0)),
        out_shape=jax.ShapeDtypeStruct(x.shape, F32),
        compiler_params=_params("parallel", "parallel"),
        name="moe_combine_dense",
    )(x, gathered, gathered, wts, mod, fnw)


def _moe(f_list, ids, wts, w1, w3, w2, layer):
    n = ids.shape[1]
    rank, counts = _moe_rank(ids)
    counts = counts[:, 0]
    padded = (counts + MOE_ROWS - 1) // MOE_ROWS * MOE_ROWS
    pad_ends = jnp.cumsum(padded)
    pad_starts = pad_ends - padded
    onehot = ids[None, :2] == jnp.arange(N_EXPERTS, dtype=I32)[:, None, None]
    dest = jnp.sum(jnp.where(onehot, pad_starts[:, None, None], 0), axis=0) + rank[:2]
    nblk = (2 * n) // MOE_ROWS + N_EXPERTS
    block_start = jnp.arange(nblk, dtype=I32) * MOE_ROWS
    block_expert = jnp.minimum(jnp.sum(block_start[:, None] >= pad_ends[None, :], axis=-1), N_EXPERTS - 1)
    n_used = (pad_ends[-1:] // MOE_ROWS).astype(I32)
    buf = jnp.zeros((nblk * MOE_ROWS, D_MODEL), F32)
    col = 0
    for f in f_list:
        f2d = f.reshape(-1, D_MODEL)
        buf = _moe_dispatch(f2d, lax.slice_in_dim(dest, col, col + f2d.shape[0], axis=1), buf)
        col += f2d.shape[0]
    y_sorted = _moe_ffn(buf, block_expert.astype(I32), n_used, w1, w3, w2, layer)
    return y_sorted, dest


def kernel(x, c, ctx, c_ctx, ada_w, ada_b, norm_mix_w, norm_ffn_w, w_in, conv_a_w, conv_a_b, hgrn_lb,
           hgrn_norm_w, ssm_conv_w, ssm_conv_b, ssm_A_log, ssm_dt_bias, ssm_D, ssm_norm_w, w_branch, w_out,
           router_group_w, router_group_b, router_expert_w, router_expert_b, moe_w1, moe_w3, moe_w2,
           final_norm_w):
    depth = ada_w.shape[0]
    bsz, t_len, _ = x.shape
    t_ctx = ctx.shape[1]
    n_lat = bsz * t_len

    cond = jnp.concatenate([c, c_ctx[None, :], jnp.zeros((8 - bsz - 1, D_MODEL), F32)], axis=0)
    mods = _ada(cond, ada_w, ada_b)
    sc = {k: jnp.asarray(v) for k, v in _scan_constants().items()}
    fft_l = tuple(jnp.asarray(a) for a in _fft_constants(t_len))
    fft_c = tuple(jnp.asarray(a) for a in _fft_constants(t_ctx))
    e64 = sc["ebd"].astype(BF16)
    lane128 = np.arange(SLAB) // (2 * HEAD)
    e128 = jnp.asarray(lane128[:, None] == lane128[None, :], BF16)
    zero_state = (jnp.zeros((bsz, 2, SLAB, SLAB), F32), jnp.zeros((bsz, 2, 2 * HEAD, SLAB), F32))
    fnw = final_norm_w.reshape(1, D_MODEL)

    for l in range(depth):
        last = l == depth - 1
        six = mods[l].reshape(8, 6, D_MODEL)
        mod_l = jnp.pad(six[:bsz], ((0, 0), (0, 2), (0, 0)))
        mod_c = jnp.broadcast_to(jnp.pad(six[bsz], ((0, 2), (0, 0))), (bsz, 8, D_MODEL))

        wl = w_in[l]
        dt0 = 12 * SLAB
        w_dt = jnp.repeat(wl[:, dt0:dt0 + 2 * N_HEADS], HEAD, axis=1)
        w_ext = jnp.concatenate([wl[:, :dt0], w_dt], axis=1).astype(BF16)
        w_gate = wl[:, dt0 + 2 * N_HEADS:].astype(BF16)
        scw, scb = ssm_conv_w[l], ssm_conv_b[l][None, :]
        caw, cab = conv_a_w[l], conv_a_b[l][None, :]
        nmix, nffn = norm_mix_w[l][None, :], norm_ffn_w[l][None, :]
        alog_lane = jnp.repeat(ssm_A_log[l], HEAD, axis=1)
        dtbias_lane = jnp.repeat(ssm_dt_bias[l], HEAD, axis=1)
        hnw = jnp.tile(hgrn_norm_w[l], N_HEADS)[None, :]
        sd = jnp.repeat(ssm_D[l], HEAD)[None, :]
        snw = ssm_norm_w[l][None, :]
        wb, wo = w_branch[l].astype(BF16), w_out[l].astype(BF16)
        wr = jnp.concatenate([router_group_w[l].T, jnp.zeros((8 - N_GROUPS, D_MODEL), F32),
                              router_expert_w[l].T], axis=0)
        br = jnp.concatenate([router_group_b[l], jnp.zeros((8 - N_GROUPS,), F32),
                              router_expert_b[l]])[:, None]
        merge_w = (nmix, nffn, w_gate, wb, wo, hnw, sd, snw, e64, e128, wr, br)

        p3_c = _inproj(ctx, mod_c, nmix, w_ext, caw, cab, scw, scb)
        *scans_c, h_state, d_state = _scan(p3_c, hgrn_lb, alog_lane, dtbias_lane, zero_state, sc, l)
        p3_l = _inproj(x, mod_l, nmix, w_ext, caw, cab, scw, scb)
        *scans_l, _, _ = _scan(p3_l, hgrn_lb, alog_lane, dtbias_lane, (h_state, d_state), sc, l)
        yc_l = _fourier(p3_l, fft_l)
        x, f_l, ids, wts = _merge(x, p3_l, scans_l, yc_l, mod_l, *merge_w)
        f_list = [f_l]
        if not last:
            yc_c = _fourier(p3_c, fft_c)
            ctx, f_c, ids_c, wts_c = _merge(ctx, p3_c, scans_c, yc_c, mod_c, *merge_w)
            f_list.append(f_c)
            ids = jnp.concatenate([ids, ids_c], axis=1)
            wts = jnp.concatenate([wts, wts_c], axis=1)
        y_sorted, dest = _moe(f_list, ids, wts, moe_w1, moe_w3, moe_w2, l)
        gathered = _sc_gather_rows(y_sorted, dest.reshape(-1)).reshape(2, -1, D_MODEL)
        x = _combine_dense(x, gathered, wts, mod_l, fnw, 0, last)
        if not last:
            ctx = _combine_dense(ctx, gathered, wts, mod_c, fnw, n_lat, False)
    return x
```

```python
import functools

import numpy as np
import jax
import jax.numpy as jnp
from jax import lax
from jax.experimental import pallas as pl
from jax.experimental.pallas import tpu as pltpu
from jax.experimental.pallas import tpu_sc as plsc

F32, BF16, I32 = jnp.float32, jnp.bfloat16, jnp.int32
HIGHEST = lax.Precision.HIGHEST
EPS = 1e-6

D_MODEL = 1024
SLAB = 256
N_HEADS = 4
HEAD = 64
C_GDIM = 64
N_BRANCH = 4
CHUNK = 64
N_LEVELS = 6
N_GROUPS, EXPERTS_PER_GROUP = 4, 8
N_EXPERTS = N_GROUPS * EXPERTS_PER_GROUP
D_FF = 512
ROUTER_ROWS = 8 + N_EXPERTS
MOE_ROWS = 512
VMEM_LIMIT = 56 * 1024 * 1024
SC_CORES, SC_SUBCORES = 2, 16
SC_WINDOW = 32

S_YA, S_Q, S_FF, S_FB, S_I, S_G, S_FOUR, S_Z, S_XS, S_BC, S_DTF, S_DTB = range(12)
N_SLABS = 12


def _sigmoid(x):
    return 1.0 / (1.0 + jnp.exp(-x))


def _softplus(x):
    return jnp.maximum(x, 0.0) + jnp.log1p(jnp.exp(-jnp.abs(x)))


def _tile(n, pref):
    t = pref
    while n % t:
        t //= 2
    return t


def _params(*sem):
    return pltpu.CompilerParams(dimension_semantics=sem, vmem_limit_bytes=VMEM_LIMIT)


def _nt(a, b, **kw):
    return lax.dot_general(a, b, (((1,), (1,)), ((), ())), preferred_element_type=F32, **kw)


def _tn(a, b, **kw):
    return lax.dot_general(a, b, (((0,), (0,)), ((), ())), preferred_element_type=F32, **kw)


def _ada_body(s_ref, w_ref, b_ref, o_ref):
    s = s_ref[...]
    s = s * _sigmoid(s)
    o_ref[...] = jnp.dot(s, w_ref[...], precision=HIGHEST, preferred_element_type=F32) + b_ref[...]


def _ada(cond, ada_w, ada_b):
    depth = ada_w.shape[0]
    n6 = ada_w.shape[2] // D_MODEL
    return pl.pallas_call(
        _ada_body,
        grid=(depth, n6),
        in_specs=[pl.BlockSpec((8, D_MODEL), lambda l, j: (0, 0)),
                  pl.BlockSpec((None, D_MODEL, D_MODEL), lambda l, j: (l, 0, j)),
                  pl.BlockSpec((None, 1, D_MODEL), lambda l, j: (l, 0, j))],
        out_specs=pl.BlockSpec((None, 8, D_MODEL), lambda l, j: (l, 0, j)),
        out_shape=jax.ShapeDtypeStruct((depth, 8, n6 * D_MODEL), F32),
        compiler_params=_params("parallel", "parallel"),
        name="ada_mod",
    )(cond, ada_w, ada_b.reshape(depth, 1, -1))


def _norm_mod(x, nw, shift, scale):
    ms = jnp.mean(x * x, axis=-1, keepdims=True)
    return (x * lax.rsqrt(ms + EPS) * nw) * (1.0 + scale) + shift


def _inproj_body(x_ref, xp_ref, xn_ref, mod_ref, nw_ref, w_ref, caw_ref, cab_ref, scw_ref, scb_ref,
                 p_ref, *, tm, nt):
    i = pl.program_id(1)
    xe = jnp.concatenate([xp_ref[...], x_ref[...], xn_ref[...]], axis=0)
    h = _norm_mod(xe, nw_ref[...], mod_ref[0:1, :], mod_ref[1:2, :])
    pr = jnp.dot(h.astype(BF16), w_ref[...], preferred_element_type=F32)
    rows = lax.broadcasted_iota(I32, (tm + 16, 1), 0)
    lo = jnp.where(i > 0, 0, 8)
    hi = jnp.where(i < nt - 1, tm + 16, tm + 8)
    vm = jnp.where((rows >= lo) & (rows < hi), 1.0, 0.0)

    def conv3(u, w, b):
        u = u * vm
        return u[7:tm + 7] * w[0:1, :] + u[8:tm + 8] * w[1:2, :] + u[9:tm + 9] * w[2:3, :] + b[...]

    ca = conv3(pr[:, SLAB:2 * SLAB] * pr[:, 2 * SLAB:3 * SLAB], caw_ref, cab_ref)
    p_ref[S_YA] = pr[8:tm + 8, 0:SLAB] * ca
    for s in range(7):
        p_ref[S_Q + s] = pr[8:tm + 8, (3 + s) * SLAB:(4 + s) * SLAB]
    cs = conv3(pr[:, 10 * SLAB:12 * SLAB], scw_ref, scb_ref)
    cs = cs * _sigmoid(cs)
    p_ref[S_XS] = cs[:, 0:SLAB]
    p_ref[S_BC] = cs[:, SLAB:2 * SLAB]
    p_ref[S_DTF] = pr[8:tm + 8, 12 * SLAB:13 * SLAB]
    p_ref[S_DTB] = pr[8:tm + 8, 13 * SLAB:14 * SLAB]


def _inproj(x, mod, nw, w_ext, caw, cab, scw, scb):
    bsz, t_len, _ = x.shape
    tm = min(512, t_len)
    nt = t_len // tm
    t8 = tm // 8
    full = lambda shape: pl.BlockSpec(shape, lambda b, i: (0,) * len(shape))
    return pl.pallas_call(
        functools.partial(_inproj_body, tm=tm, nt=nt),
        grid=(bsz, nt),
        in_specs=[pl.BlockSpec((None, tm, D_MODEL), lambda b, i: (b, i, 0)),
                  pl.BlockSpec((None, 8, D_MODEL), lambda b, i: (b, jnp.maximum(i * t8 - 1, 0), 0)),
                  pl.BlockSpec((None, 8, D_MODEL), lambda b, i: (b, jnp.minimum((i + 1) * t8, t_len // 8 - 1), 0)),
                  pl.BlockSpec((None, 8, D_MODEL), lambda b, i: (b, 0, 0)),
                  full((1, D_MODEL)), full(w_ext.shape), full(caw.shape), full(cab.shape),
                  full(scw.shape), full(scb.shape)],
        out_specs=pl.BlockSpec((N_SLABS, None, tm, SLAB), lambda b, i: (0, b, i, 0)),
        out_shape=jax.ShapeDtypeStruct((N_SLABS, bsz, t_len, SLAB), F32),
        compiler_params=_params("parallel", "parallel"),
        name="in_proj",
    )(x, x, x, mod, nw, w_ext, caw, cab, scw, scb)


def _scan_constants():
    t = np.arange(CHUNK)[:, None]
    r = np.arange(CHUNK)[None, :]
    out = {}
    for name, fwd in (("f", True), ("b", False)):
        blocks = [(r <= t) if fwd else (r >= t), (r > t) if fwd else (r < t)]
        qs, ks, masks = [], [], []
        for lvl in range(N_LEVELS):
            m = 1 << lvl
            blk = t // (2 * m)
            ref = blk * 2 * m + m - 1
            upper = (t % (2 * m)) >= m
            s_blk = (r // (2 * m))
            s_upper = (r % (2 * m)) >= m
            if fwd:
                qs.append(upper & (r > ref) & (r <= t))
                ks.append((~upper) & (r > t) & (r <= ref))
                masks.append(upper & (~s_upper) & (s_blk == blk))
            else:
                qs.append((~upper) & (r >= t) & (r <= ref))
                ks.append(upper & (r > ref) & (r < t))
                masks.append((~upper) & s_upper & (s_blk == blk))
        out["cm_" + name] = np.concatenate(blocks + [q | k for q, k in zip(qs, ks)], axis=0).astype(np.float32)
        out["lm_" + name] = np.stack([np.tile(mk, (1, N_HEADS)) for mk in masks]).astype(np.float32)
        out["su_" + name] = np.tile((r.T > r) if fwd else (r.T < r), (1, N_HEADS)).astype(np.float32)
        out["sc_" + name] = np.tile((t >= r) if fwd else (r >= t), (1, N_HEADS)).astype(np.float32)
    lane_head = np.arange(SLAB) // HEAD
    out["ebd"] = (lane_head[:, None] == lane_head[None, :]).astype(np.float32)
    lane_group = np.arange(2 * HEAD) // HEAD
    out["gm4"] = (lane_head[:, None] // 2 == lane_group[None, :]).astype(np.float32)
    return out


def _split_bf16(a):
    hi = a.astype(BF16)
    return hi, (a - hi.astype(F32)).astype(BF16)


def _head_blocks(a, mask):
    return jnp.concatenate([a.astype(BF16)] * N_HEADS, axis=0) * mask


def _hgrn_chunk(qv, kv, vv, logf, s_ref, k, cm, lm_ref, ebd, fwd):
    c = CHUNK
    e = jnp.exp(jnp.dot(cm, jnp.concatenate(_split_bf16(logf), axis=0), preferred_element_type=F32))
    eb = e.astype(BF16)
    qb, kb = qv.astype(BF16), kv.astype(BF16)
    p = jnp.zeros((c, SLAB), F32)
    for lvl in range(N_LEVELS):
        el = eb[(2 + lvl) * c:(3 + lvl) * c]
        p = p + _nt(qb * el, _head_blocks(kb * el, ebd)) * lm_ref[lvl]
    state = s_ref[k]
    o = jnp.dot((qv * kv).astype(BF16), ebd, preferred_element_type=F32) * vv
    o = o + jnp.dot(p.astype(BF16), _head_blocks(vv, ebd), preferred_element_type=F32)
    o = o + _nt((qv * e[0:c]).astype(BF16), state.astype(BF16))
    upd = _tn(vv.astype(BF16), (kv * e[c:2 * c]).astype(BF16))
    tot = e[c - 1:c] if fwd else e[0:1]
    s_ref[k] = state * tot + upd * ebd.astype(F32)
    return o


def _ssd_chunk(bc, xv, dt, a, s_ref, k, sm, um, caus, gm4, gms, ebd, fwd):
    c = CHUNK
    bv, cv = bc[:, :2 * HEAD], bc[:, 2 * HEAD:].astype(BF16)
    a_hi, a_lo = _split_bf16(a)
    ex = jnp.dot(sm, jnp.concatenate([a_hi, a_lo], axis=0), preferred_element_type=F32)
    cum, aft = ex[0:c], ex[c:]
    between = jnp.dot(sm[0:c], jnp.concatenate([a_hi * um, a_lo * um], axis=0),
                      preferred_element_type=F32)
    p = _nt(cv, _head_blocks(bv, gm4)) * (jnp.exp(between) * caus)
    xt = xv * dt
    state = s_ref[k]
    o = jnp.dot(p.astype(BF16), _head_blocks(xt, ebd), preferred_element_type=F32)
    o = o + jnp.dot(cv, state.astype(BF16), preferred_element_type=F32) * jnp.exp(cum)
    upd = _tn(bv.astype(BF16), (xt * jnp.exp(aft)).astype(BF16))
    tot = jnp.exp(cum[c - 1:c] if fwd else cum[0:1])
    s_ref[k] = state * tot + upd * gms
    return o


def _scan_body(qf, ff, vf, xsf, bcf, dtf, qb, fb, vb, xsb, bcb, dtb,
               lb_ref, alog_ref, dtbias_ref, cmf_ref, cmb_ref, lmf_ref, lmb_ref, suf_ref, sub_ref, scf_ref, scb_ref,
               gm4_ref, gms_ref, ebd_ref, h0_ref, d0_ref,
               ohf, ohb, osf, osb, hout_ref, dout_ref, sh_ref, sd_ref, *, layer, nck, nsteps):
    i = pl.program_id(1)

    @pl.when(i == 0)
    def _():
        sh_ref[...] = h0_ref[...]
        sd_ref[...] = d0_ref[...]

    ebd = ebd_ref[...]
    cmf, cmb = cmf_ref[...], cmb_ref[...]
    smf, smb = cmf_ref[0:2 * CHUNK, :], cmb_ref[0:2 * CHUNK, :]

    def lower_bound(d):
        rows = lb_ref[d]
        ex = jnp.exp(rows - jnp.max(rows, axis=0, keepdims=True))
        prob = ex / jnp.sum(ex, axis=0, keepdims=True)
        lb = jnp.zeros((1, SLAB), F32)
        for j in range(1, layer + 1):
            lb = lb + prob[j:j + 1]
        return lb

    lbs = [lower_bound(0), lower_bound(1)]

    def hgrn_inputs(q_raw, f_raw, lb):
        q = q_raw * _sigmoid(q_raw)
        log_sig = jnp.minimum(f_raw, 0.0) - jnp.log1p(jnp.exp(-jnp.abs(f_raw)))
        a = jnp.log(lb)
        b = jnp.log1p(-lb) + log_sig
        logf = jnp.maximum(a, b) + jnp.log1p(jnp.exp(-jnp.abs(a - b)))
        kk = (1.0 - lb) * _sigmoid(-f_raw)
        return q, kk, logf

    def ssd_inputs(dt_raw, d):
        dt = _softplus(dt_raw + dtbias_ref[d:d + 1, :])
        return dt, -jnp.exp(alog_ref[d:d + 1, :]) * dt

    def body(j, carry):
        rf = pl.ds(pl.multiple_of(j * CHUNK, CHUNK), CHUNK)
        rb = pl.ds(pl.multiple_of((nck - 1 - j) * CHUNK, CHUNK), CHUNK)
        q, kk, logf = hgrn_inputs(qf[rf, :], ff[rf, :], lbs[0])
        ohf[rf, :] = _hgrn_chunk(q, kk, vf[rf, :], logf, sh_ref, 0, cmf, lmf_ref, ebd, True)
        q, kk, logf = hgrn_inputs(qb[rb, :], fb[rb, :], lbs[1])
        ohb[rb, :] = _hgrn_chunk(q, kk, vb[rb, :], logf, sh_ref, 1, cmb, lmb_ref, ebd, False)
        dt, a = ssd_inputs(dtf[rf, :], 0)
        osf[rf, :] = _ssd_chunk(bcf[rf, :], xsf[rf, :], dt, a, sd_ref, 0, smf, suf_ref[...], scf_ref[...],
                                gm4_ref[...], gms_ref[...], ebd, True)
        dt, a = ssd_inputs(dtb[rb, :], 1)
        osb[rb, :] = _ssd_chunk(bcb[rb, :], xsb[rb, :], dt, a, sd_ref, 1, smb, sub_ref[...], scb_ref[...],
                                gm4_ref[...], gms_ref[...], ebd, False)
        return carry

    lax.fori_loop(0, nck, body, 0, unroll=2)

    @pl.when(i == nsteps - 1)
    def _():
        hout_ref[...] = sh_ref[...]
        dout_ref[...] = sd_ref[...]


def _scan(p3, hgrn_lb, alog_lane, dtbias_lane, s0, consts, layer):
    _, bsz, t_len, _ = p3.shape
    cb = min(256, t_len)
    nb = t_len // cb
    slab_f = lambda s: pl.BlockSpec((None, None, cb, SLAB), lambda b, i: (s, b, i, 0))
    slab_b = lambda s: pl.BlockSpec((None, None, cb, SLAB), lambda b, i: (s, b, nb - 1 - i, 0))
    full = lambda a: pl.BlockSpec(a.shape, lambda b, i: (0,) * a.ndim)
    bf = lambda name: consts[name].astype(BF16)
    cm2 = lambda name: jnp.concatenate([bf(name), bf(name)], axis=1)
    small = [hgrn_lb, alog_lane, dtbias_lane, cm2("cm_f"), cm2("cm_b"), consts["lm_f"], consts["lm_b"],
             bf("su_f"), bf("su_b"), consts["sc_f"], consts["sc_b"], bf("gm4"), consts["gm4"].T, bf("ebd")]
    o_f = pl.BlockSpec((None, cb, SLAB), lambda b, i: (b, i, 0))
    o_b = pl.BlockSpec((None, cb, SLAB), lambda b, i: (b, nb - 1 - i, 0))
    st_h = pl.BlockSpec((None, 2, SLAB, SLAB), lambda b, i: (b, 0, 0, 0))
    st_d = pl.BlockSpec((None, 2, 2 * HEAD, SLAB), lambda b, i: (b, 0, 0, 0))
    o_shape = jax.ShapeDtypeStruct((bsz, t_len, SLAB), F32)
    h0, d0 = s0
    return pl.pallas_call(
        functools.partial(_scan_body, layer=layer, nck=cb // CHUNK, nsteps=nb),
        grid=(bsz, nb),
        in_specs=([slab_f(s) for s in (S_Q, S_FF, S_I, S_XS, S_BC, S_DTF)]
                  + [slab_b(s) for s in (S_Q, S_FB, S_I, S_XS, S_BC, S_DTB)]
                  + [full(a) for a in small] + [st_h, st_d]),
        out_specs=[o_f, o_b, o_f, o_b, st_h, st_d],
        out_shape=[o_shape, o_shape, o_shape, o_shape, jax.ShapeDtypeStruct(h0.shape, F32),
                   jax.ShapeDtypeStruct(d0.shape, F32)],
        scratch_shapes=[pltpu.VMEM((2, SLAB, SLAB), F32), pltpu.VMEM((2, 2 * HEAD, SLAB), F32)],
        compiler_params=_params("parallel", "arbitrary"),
        name="chunk_scan",
    )(*([p3] * 12), *small, h0, d0)


def _fft_sizes(t_len):
    tf = 64 if t_len >= 4096 else 16
    return tf, t_len // tf


def _fft_constants(t_len):
    tf, ts = _fft_sizes(t_len)
    c = np.arange(SLAB)
    same = (c[:, None] // C_GDIM) == (c[None, :] // C_GDIM)
    ang = 2 * np.pi * ((c[:, None] % C_GDIM) * (c[None, :] % C_GDIM) % C_GDIM) / C_GDIM
    chan = np.concatenate([np.cos(ang) * same, -np.sin(ang) * same], axis=1)
    k2 = np.arange(ts)
    a = 2 * np.pi * (k2[:, None] * k2[None, :] % ts) / ts
    flat = np.block([[np.cos(a), np.sin(a)], [-np.sin(a), np.cos(a)]])
    k1 = np.arange(tf)[None, :, None]
    tfi = np.arange(tf)[None, None, :]
    kk2 = np.arange(ts)[:, None, None]
    th = 2 * np.pi * (((tfi * k1 * ts) + tfi * kk2) % t_len) / t_len
    scale = 1.0 / np.sqrt(t_len * C_GDIM)
    slab = np.concatenate([np.cos(th), np.sin(th)], axis=2) * scale
    return chan.astype(np.float32), flat.astype(np.float32), slab.astype(np.float32)


def _fft_flat_body(x_ref, chan_ref, flat_ref, y_ref, *, ts, groups):
    for g in range(groups):
        xg = x_ref[:, g * SLAB:(g + 1) * SLAB]
        a = jnp.dot(xg.astype(BF16), chan_ref[...], preferred_element_type=F32)
        z = jnp.concatenate([a[:, :SLAB], a[:, SLAB:]], axis=0).astype(BF16)
        y = jnp.dot(flat_ref[...], z, preferred_element_type=F32)
        y_ref[0, :, g * SLAB:(g + 1) * SLAB] = y[:ts]
        y_ref[1, :, g * SLAB:(g + 1) * SLAB] = y[ts:]


def _fft_slab_body(y_ref, m_ref, o_ref, *, kb):
    for j in range(kb):
        z = jnp.concatenate([y_ref[0, j], y_ref[1, j]], axis=0).astype(BF16)
        o_ref[j] = jnp.dot(m_ref[j], z, preferred_element_type=F32)


def _fourier(p3, consts):
    _, bsz, t_len, _ = p3.shape
    tf, ts = _fft_sizes(t_len)
    chan, flat, slab = (a.astype(BF16) for a in consts)
    groups = min(8, tf)
    lanes = groups * SLAB
    y = pl.pallas_call(
        functools.partial(_fft_flat_body, ts=ts, groups=groups),
        grid=(bsz, tf // groups),
        in_specs=[pl.BlockSpec((None, ts, lanes), lambda b, j: (b, 0, j)),
                  pl.BlockSpec(chan.shape, lambda b, j: (0, 0)),
                  pl.BlockSpec(flat.shape, lambda b, j: (0, 0))],
        out_specs=pl.BlockSpec((None, 2, ts, lanes), lambda b, j: (b, 0, 0, j)),
        out_shape=jax.ShapeDtypeStruct((bsz, 2, ts, tf * SLAB), F32),
        compiler_params=_params("parallel", "parallel"),
        name="fft_flat",
    )(p3[S_FOUR].reshape(bsz, ts, tf * SLAB), chan, flat)
    kb = min(8, ts)
    out = pl.pallas_call(
        functools.partial(_fft_slab_body, kb=kb),
        grid=(bsz, ts // kb),
        in_specs=[pl.BlockSpec((None, 2, kb, tf, SLAB), lambda b, j: (b, 0, j, 0, 0)),
                  pl.BlockSpec((kb, tf, 2 * tf), lambda b, j: (j, 0, 0))],
        out_specs=pl.BlockSpec((None, kb, tf, SLAB), lambda b, j: (b, j, 0, 0)),
        out_shape=jax.ShapeDtypeStruct((bsz, ts, tf, SLAB), F32),
        compiler_params=_params("parallel", "parallel"),
        name="fft_slab",
    )(y.reshape(bsz, 2, ts, tf, SLAB), slab)
    return out.transpose(0, 2, 1, 3).reshape(bsz, t_len, SLAB)


def _merge_body(x_ref, ya_ref, g_ref, z_ref, xs_ref, hf_ref, hb_ref, sf_ref, sb_ref, yc_ref,
                mod_ref, nmix_ref, nffn_ref, wg_ref, wb_ref, wo_ref, hnw_ref, sd_ref, snw_ref,
                e64_ref, e128_ref, wr_ref, br_ref, f_prev_ref,
                xo_ref, f_ref, ids_ref, wts_ref):
    del f_prev_ref
    x = x_ref[...]
    h = _norm_mod(x, nmix_ref[...], mod_ref[0:1, :], mod_ref[1:2, :]).astype(BF16)

    def group_rms(v, e_ref, width):
        ms = jnp.dot((v * v).astype(BF16), e_ref[...], preferred_element_type=F32) * (1.0 / width)
        return v * lax.rsqrt(ms + EPS)

    g = g_ref[...]
    y_b = group_rms(hf_ref[...] + hb_ref[...], e64_ref, HEAD) * hnw_ref[...] * (g * _sigmoid(g))
    z = z_ref[...]
    y_d = (sf_ref[...] + sb_ref[...] + sd_ref[...] * xs_ref[...]) * (z * _sigmoid(z))
    y_d = group_rms(y_d, e128_ref, 2 * HEAD) * snw_ref[...]
    branches = (ya_ref[...], y_b, yc_ref[...], y_d)
    merged = None
    for kb in range(N_BRANCH):
        gate = _sigmoid(jnp.dot(h, wg_ref[:, kb * D_MODEL:(kb + 1) * D_MODEL], preferred_element_type=F32))
        term = gate * jnp.dot(branches[kb].astype(BF16), wb_ref[kb], preferred_element_type=F32)
        merged = term if merged is None else merged + term
    mix = jnp.dot(merged.astype(BF16), wo_ref[...], preferred_element_type=F32)
    xo = x + mod_ref[2:3, :] * mix
    xo_ref[...] = xo
    f = _norm_mod(xo, nffn_ref[...], mod_ref[3:4, :], mod_ref[4:5, :])
    f_ref[...] = f

    lg = _nt(wr_ref[...], f, precision=HIGHEST) + br_ref[...]
    gl = lg[0:N_GROUPS]
    gidx = lax.broadcasted_iota(I32, gl.shape, 0)
    gmax = jnp.max(gl, axis=0, keepdims=True)
    g_top = jnp.min(jnp.where(gl == gmax, gidx, N_GROUPS), axis=0, keepdims=True)
    p_group = 1.0 / jnp.sum(jnp.exp(gl - gmax), axis=0, keepdims=True)
    e_in = jnp.zeros((EXPERTS_PER_GROUP, gl.shape[1]), F32)
    for grp in range(N_GROUPS):
        e_in = jnp.where(g_top == grp, lg[8 + 8 * grp:16 + 8 * grp], e_in)
    eidx = lax.broadcasted_iota(I32, e_in.shape, 0)
    v1 = jnp.max(e_in, axis=0, keepdims=True)
    i1 = jnp.min(jnp.where(e_in == v1, eidx, EXPERTS_PER_GROUP), axis=0, keepdims=True)
    rest = jnp.where(eidx == i1, -jnp.inf, e_in)
    v2 = jnp.max(rest, axis=0, keepdims=True)
    i2 = jnp.min(jnp.where(rest == v2, eidx, EXPERTS_PER_GROUP), axis=0, keepdims=True)
    w1 = 1.0 / (1.0 + jnp.exp(v2 - v1))
    row = lax.broadcasted_iota(I32, (8, gl.shape[1]), 0)
    base = g_top * EXPERTS_PER_GROUP
    ids_ref[...] = jnp.where(row == 0, base + i1, jnp.where(row == 1, base + i2, 0))
    wts_ref[...] = jnp.where(row == 0, p_group * w1, jnp.where(row == 1, p_group * (1.0 - w1), 0.0))


def _merge_body_first(*refs):
    n_in = len(refs) - 4
    return _merge_body(*refs[:n_in], None, *refs[n_in:])


def _merge(x, p3, scans, y_c, mod, nmix, nffn, wg, wb, wo, hnw, sd, snw, e64, e128, wr, br, f_rows, f_prev, row0):
    bsz, t_len, _ = x.shape
    tm = min(512, t_len)
    nt = t_len // tm
    r0 = row0 // tm
    row = lambda w: pl.BlockSpec((None, tm, w), lambda b, i: (b, i, 0))
    slab = lambda s: pl.BlockSpec((None, None, tm, SLAB), lambda b, i: (s, b, i, 0))
    full = lambda a: pl.BlockSpec(a.shape, lambda b, i: (0,) * a.ndim, pipeline_mode=pl.Buffered(1))
    consts = [nmix, nffn, wg, wb, wo, hnw, sd, snw, e64, e128, wr, br]
    lane_out = pl.BlockSpec((8, tm), lambda b, i: (0, b * nt + i))
    in_specs = ([row(D_MODEL)] + [slab(s) for s in (S_YA, S_G, S_Z, S_XS)] + [row(SLAB)] * 5
                + [pl.BlockSpec((None, 8, D_MODEL), lambda b, i: (b, 0, 0))] + [full(a) for a in consts])
    operands = [x, p3, p3, p3, p3, *scans, y_c, mod, *consts]
    aliases = {}
    if f_prev is not None:
        in_specs.append(pl.BlockSpec(memory_space=pl.ANY))
        aliases = {len(operands): 1}
        operands.append(f_prev)
    return pl.pallas_call(
        _merge_body if f_prev is not None else _merge_body_first,
        grid=(bsz, nt),
        in_specs=in_specs,
        out_specs=[row(D_MODEL), pl.BlockSpec((tm, D_MODEL), lambda b, i: (r0 + b * nt + i, 0)), lane_out, lane_out],
        out_shape=[jax.ShapeDtypeStruct(x.shape, F32), jax.ShapeDtypeStruct((f_rows, D_MODEL), F32),
                   jax.ShapeDtypeStruct((8, bsz * t_len), I32), jax.ShapeDtypeStruct((8, bsz * t_len), F32)],
        input_output_aliases=aliases,
        compiler_params=_params("parallel", "parallel"),
        name="merge_router",
    )(*operands)


def _rank_body(ids_ref, tri_ref, rank_ref, cnt_ref, carry_ref, *, nsteps):
    i = pl.program_id(0)

    @pl.when(i == 0)
    def _():
        carry_ref[...] = jnp.zeros_like(carry_ref)

    tr = ids_ref.shape[1]
    eidx = lax.broadcasted_iota(I32, (N_EXPERTS, tr), 0)
    oh0 = jnp.where(eidx == ids_ref[0:1, :], 1.0, 0.0)
    oh1 = jnp.where(eidx == ids_ref[1:2, :], 1.0, 0.0)
    both = oh0 + oh1
    before = jnp.dot(both.astype(BF16), tri_ref[...], preferred_element_type=F32)
    pos = before + carry_ref[...]
    r0 = jnp.sum(oh0 * pos, axis=0, keepdims=True)
    r1 = jnp.sum(oh1 * pos, axis=0, keepdims=True)
    row = lax.broadcasted_iota(I32, (8, tr), 0)
    rank_ref[...] = jnp.where(row == 0, r0, jnp.where(row == 1, r1, 0.0)).astype(I32)
    carry_ref[...] = carry_ref[...] + jnp.sum(both, axis=1, keepdims=True)

    @pl.when(i == nsteps - 1)
    def _():
        cnt_ref[...] = carry_ref[...].astype(I32)


def _moe_rank(ids):
    n = ids.shape[1]
    tr = _tile(n, 1024)
    tri = jnp.triu(jnp.ones((tr, tr), BF16), k=1)
    return pl.pallas_call(
        functools.partial(_rank_body, nsteps=n // tr),
        grid=(n // tr,),
        in_specs=[pl.BlockSpec((8, tr), lambda i: (0, i)), pl.BlockSpec((tr, tr), lambda i: (0, 0))],
        out_specs=[pl.BlockSpec((8, tr), lambda i: (0, i)), pl.BlockSpec((N_EXPERTS, 1), lambda i: (0, 0))],
        out_shape=[jax.ShapeDtypeStruct((8, n), I32), jax.ShapeDtypeStruct((N_EXPERTS, 1), I32)],
        scratch_shapes=[pltpu.VMEM((N_EXPERTS, 1), F32)],
        compiler_params=_params("arbitrary"),
        name="moe_rank",
    )(ids, tri)


def _ffn_body(be_ref, nu_ref, x_ref, w1_ref, w3_ref, w2_ref, o_ref, w1b_ref, w3b_ref, w2b_ref):
    i = pl.program_id(0)
    used = i < nu_ref[0]
    fresh = (i == 0) | (be_ref[i] != be_ref[jnp.maximum(i - 1, 0)])

    @pl.when(used & fresh)
    def _():
        w1b_ref[...] = w1_ref[...].astype(BF16)
        w3b_ref[...] = w3_ref[...].astype(BF16)
        w2b_ref[...] = w2_ref[...].astype(BF16)

    @pl.when(used)
    def _():
        xb = x_ref[...].astype(BF16)
        a = jnp.dot(xb, w1b_ref[...], preferred_element_type=F32)
        b = jnp.dot(xb, w3b_ref[...], preferred_element_type=F32)
        hid = (a * _sigmoid(a) * b).astype(BF16)
        o_ref[...] = jnp.dot(hid, w2b_ref[...], preferred_element_type=F32)

    @pl.when(jnp.logical_not(used))
    def _():
        o_ref[...] = jnp.zeros_like(o_ref)


def _moe_ffn(buf, block_expert, n_used, w1, w3, w2, layer):
    nblk = buf.shape[0] // MOE_ROWS
    wspec = lambda shape: pl.BlockSpec((None, None) + shape, lambda i, be, nu: (layer, be[i], 0, 0))
    return pl.pallas_call(
        _ffn_body,
        grid_spec=pltpu.PrefetchScalarGridSpec(
            num_scalar_prefetch=2,
            grid=(nblk,),
            in_specs=[pl.BlockSpec((MOE_ROWS, D_MODEL), lambda i, be, nu: (i, 0)),
                      wspec((D_MODEL, D_FF)), wspec((D_MODEL, D_FF)), wspec((D_FF, D_MODEL))],
            out_specs=pl.BlockSpec((MOE_ROWS, D_MODEL), lambda i, be, nu: (i, 0)),
            scratch_shapes=[pltpu.VMEM((D_MODEL, D_FF), BF16), pltpu.VMEM((D_MODEL, D_FF), BF16),
                            pltpu.VMEM((D_FF, D_MODEL), BF16)]),
        out_shape=jax.ShapeDtypeStruct(buf.shape, F32),
        compiler_params=_params("arbitrary"),
        name="moe_ffn",
    )(block_expert, n_used, buf, w1, w3, w2)


def _sc_gather_rows(src, idx):
    m, width = idx.shape[0], src.shape[1]
    mesh = plsc.VectorSubcoreMesh(core_axis_name="c", subcore_axis_name="s",
                                  num_cores=SC_CORES, num_subcores=SC_SUBCORES)

    @functools.partial(pl.kernel, out_type=jax.ShapeDtypeStruct((m, width), src.dtype), mesh=mesh,
                       scratch_types=[], name="sc_gather_rows",
                       compiler_params=pltpu.CompilerParams(use_tc_tiling_on_sc=True))
    def gather(src_hbm, idx_hbm, out_hbm):
        def body(idx_vmem, out_vmem):
            pltpu.sync_copy(src_hbm.at[idx_vmem.at[0]], out_vmem)

        pltpu.emit_pipeline(
            body,
            grid=(m // SC_WINDOW,),
            in_specs=[pl.BlockSpec((None, 1, SC_WINDOW), lambda i: (i, 0, 0))],
            out_specs=[pl.BlockSpec((SC_WINDOW, width), lambda i: (i, 0))],
            core_axis_name=("c", "s"),
            dimension_semantics=(pltpu.PARALLEL,),
        )(idx_hbm, out_hbm)

    return gather(src, idx.reshape(m // SC_WINDOW, 1, SC_WINDOW))


def _combine_dense_body(x_ref, g0_ref, g1_ref, wts_ref, mod_ref, fnw_ref, o_ref, *, tc, final):
    eye = jnp.where(lax.broadcasted_iota(I32, (tc, tc), 0) == lax.broadcasted_iota(I32, (tc, tc), 1), 1.0, 0.0)
    wcol = _nt(eye, wts_ref[...], precision=HIGHEST)
    y = wcol[:, 0:1] * g0_ref[...] + wcol[:, 1:2] * g1_ref[...]
    out = x_ref[...] + mod_ref[5:6, :] * y
    if final:
        ms = jnp.mean(out * out, axis=-1, keepdims=True)
        out = out * lax.rsqrt(ms + EPS) * fnw_ref[...]
    o_ref[...] = out


def _combine_dense(x, gathered, wts, mod, fnw, col0, final):
    bsz, t_len, _ = x.shape
    tc = min(256, t_len)
    nt = t_len // tc
    c0 = col0 // tc
    g_spec = lambda k: pl.BlockSpec((None, tc, D_MODEL), lambda b, i: (k, c0 + b * nt + i, 0))
    return pl.pallas_call(
        functools.partial(_combine_dense_body, tc=tc, final=final),
        grid=(bsz, nt),
        in_specs=[pl.BlockSpec((None, tc, D_MODEL), lambda b, i: (b, i, 0)), g_spec(0), g_spec(1),
                  pl.BlockSpec((8, tc), lambda b, i: (0, c0 + b * nt + i)),
                  pl.BlockSpec((None, 8, D_MODEL), lambda b, i: (b, 0, 0)),
                  pl.BlockSpec((1, D_MODEL), lambda b, i: (0, 0))],
        out_specs=pl.BlockSpec((None, tc, D_MODEL), lambda b, i: (b, i, 0)),
        out_shape=jax.ShapeDtypeStruct(x.shape, F32),
        compiler_params=_params("parallel", "parallel"),
        name="moe_combine_dense",
    )(x, gathered, gathered, wts, mod, fnw)


def _moe(f_all, ids, w1, w3, w2, layer):
    n = ids.shape[1]
    rank, counts = _moe_rank(ids)
    counts = counts[:, 0]
    padded = (counts + MOE_ROWS - 1) // MOE_ROWS * MOE_ROWS
    pad_ends = jnp.cumsum(padded)
    pad_starts = pad_ends - padded
    onehot = ids[None, :2] == jnp.arange(N_EXPERTS, dtype=I32)[:, None, None]
    dest = jnp.sum(jnp.where(onehot, pad_starts[:, None, None], 0), axis=0) + rank[:2]
    nblk = (2 * n) // MOE_ROWS + N_EXPERTS
    block_start = jnp.arange(nblk, dtype=I32) * MOE_ROWS
    block_expert = jnp.minimum(jnp.sum(block_start[:, None] >= pad_ends[None, :], axis=-1), N_EXPERTS - 1)
    n_used = (pad_ends[-1:] // MOE_ROWS).astype(I32)
    token = jnp.tile(jnp.arange(n, dtype=I32), 2)
    src = jnp.zeros((nblk * MOE_ROWS,), I32).at[dest.reshape(-1)].set(token, unique_indices=True)
    buf = _sc_gather_rows(f_all, src)
    y_sorted = _moe_ffn(buf, block_expert.astype(I32), n_used, w1, w3, w2, layer)
    return y_sorted, dest


def kernel(x, c, ctx, c_ctx, ada_w, ada_b, norm_mix_w, norm_ffn_w, w_in, conv_a_w, conv_a_b, hgrn_lb,
           hgrn_norm_w, ssm_conv_w, ssm_conv_b, ssm_A_log, ssm_dt_bias, ssm_D, ssm_norm_w, w_branch, w_out,
           router_group_w, router_group_b, router_expert_w, router_expert_b, moe_w1, moe_w3, moe_w2,
           final_norm_w):
    depth = ada_w.shape[0]
    bsz, t_len, _ = x.shape
    t_ctx = ctx.shape[1]
    n_lat = bsz * t_len

    cond = jnp.concatenate([c, c_ctx[None, :], jnp.zeros((8 - bsz - 1, D_MODEL), F32)], axis=0)
    mods = _ada(cond, ada_w, ada_b)
    sc = {k: jnp.asarray(v) for k, v in _scan_constants().items()}
    fft_l = tuple(jnp.asarray(a) for a in _fft_constants(t_len))
    fft_c = tuple(jnp.asarray(a) for a in _fft_constants(t_ctx))
    e64 = sc["ebd"].astype(BF16)
    lane128 = np.arange(SLAB) // (2 * HEAD)
    e128 = jnp.asarray(lane128[:, None] == lane128[None, :], BF16)
    zero_state = (jnp.zeros((bsz, 2, SLAB, SLAB), F32), jnp.zeros((bsz, 2, 2 * HEAD, SLAB), F32))
    fnw = final_norm_w.reshape(1, D_MODEL)

    for l in range(depth):
        last = l == depth - 1
        six = mods[l].reshape(8, 6, D_MODEL)
        mod_l = jnp.pad(six[:bsz], ((0, 0), (0, 2), (0, 0)))
        mod_c = jnp.broadcast_to(jnp.pad(six[bsz], ((0, 2), (0, 0))), (bsz, 8, D_MODEL))

        wl = w_in[l]
        dt0 = 12 * SLAB
        w_dt = jnp.repeat(wl[:, dt0:dt0 + 2 * N_HEADS], HEAD, axis=1)
        w_ext = jnp.concatenate([wl[:, :dt0], w_dt], axis=1).astype(BF16)
        w_gate = wl[:, dt0 + 2 * N_HEADS:].astype(BF16)
        scw, scb = ssm_conv_w[l], ssm_conv_b[l][None, :]
        caw, cab = conv_a_w[l], conv_a_b[l][None, :]
        nmix, nffn = norm_mix_w[l][None, :], norm_ffn_w[l][None, :]
        alog_lane = jnp.repeat(ssm_A_log[l], HEAD, axis=1)
        dtbias_lane = jnp.repeat(ssm_dt_bias[l], HEAD, axis=1)
        hnw = jnp.tile(hgrn_norm_w[l], N_HEADS)[None, :]
        sd = jnp.repeat(ssm_D[l], HEAD)[None, :]
        snw = ssm_norm_w[l][None, :]
        wb, wo = w_branch[l].astype(BF16), w_out[l].astype(BF16)
        wr = jnp.concatenate([router_group_w[l].T, jnp.zeros((8 - N_GROUPS, D_MODEL), F32),
                              router_expert_w[l].T], axis=0)
        br = jnp.concatenate([router_group_b[l], jnp.zeros((8 - N_GROUPS,), F32),
                              router_expert_b[l]])[:, None]
        merge_w = (nmix, nffn, w_gate, wb, wo, hnw, sd, snw, e64, e128, wr, br)

        p3_c = _inproj(ctx, mod_c, nmix, w_ext, caw, cab, scw, scb)
        *scans_c, h_state, d_state = _scan(p3_c, hgrn_lb, alog_lane, dtbias_lane, zero_state, sc, l)
        p3_l = _inproj(x, mod_l, nmix, w_ext, caw, cab, scw, scb)
        *scans_l, _, _ = _scan(p3_l, hgrn_lb, alog_lane, dtbias_lane, (h_state, d_state), sc, l)
        yc_l = _fourier(p3_l, fft_l)
        f_rows = n_lat if last else n_lat + bsz * t_ctx
        x, f_all, ids, wts = _merge(x, p3_l, scans_l, yc_l, mod_l, *merge_w, f_rows, None, 0)
        if not last:
            yc_c = _fourier(p3_c, fft_c)
            ctx, f_all, ids_c, wts_c = _merge(ctx, p3_c, scans_c, yc_c, mod_c, *merge_w, f_rows, f_all, n_lat)
            ids = jnp.concatenate([ids, ids_c], axis=1)
            wts = jnp.concatenate([wts, wts_c], axis=1)
        y_sorted, dest = _moe(f_all, ids, moe_w1, moe_w3, moe_w2, l)
        gathered = _sc_gather_rows(y_sorted, dest.reshape(-1)).reshape(2, -1, D_MODEL)
        x = _combine_dense(x, gathered, wts, mod_l, fnw, 0, last)
        if not last:
            ctx = _combine_dense(ctx, gathered, wts, mod_c, fnw, n_lat, False)
    return x
```

```python
import functools

import numpy as np
import jax
import jax.numpy as jnp
from jax import lax
from jax.experimental import pallas as pl
from jax.experimental.pallas import tpu as pltpu
from jax.experimental.pallas import tpu_sc as plsc

F32, BF16, I32 = jnp.float32, jnp.bfloat16, jnp.int32
HIGHEST = lax.Precision.HIGHEST
EPS = 1e-6

D_MODEL = 1024
SLAB = 256
N_HEADS = 4
HEAD = 64
C_GDIM = 64
N_BRANCH = 4
CHUNK = 64
N_LEVELS = 6
N_GROUPS, EXPERTS_PER_GROUP = 4, 8
N_EXPERTS = N_GROUPS * EXPERTS_PER_GROUP
D_FF = 512
ROUTER_ROWS = 8 + N_EXPERTS
MOE_ROWS = 512
VMEM_LIMIT = 56 * 1024 * 1024
SC_CORES, SC_SUBCORES = 2, 16
SC_WINDOW = 32

S_YA, S_Q, S_FF, S_FB, S_I, S_G, S_FOUR, S_Z, S_XS, S_BC, S_DTF, S_DTB = range(12)
N_SLABS = 12


def _sigmoid(x):
    return 1.0 / (1.0 + jnp.exp(-x))


def _softplus(x):
    return jnp.maximum(x, 0.0) + jnp.log1p(jnp.exp(-jnp.abs(x)))


def _tile(n, pref):
    t = pref
    while n % t:
        t //= 2
    return t


def _params(*sem):
    return pltpu.CompilerParams(dimension_semantics=sem, vmem_limit_bytes=VMEM_LIMIT)


def _nt(a, b, **kw):
    return lax.dot_general(a, b, (((1,), (1,)), ((), ())), preferred_element_type=F32, **kw)


def _tn(a, b, **kw):
    return lax.dot_general(a, b, (((0,), (0,)), ((), ())), preferred_element_type=F32, **kw)


def _ada_body(s_ref, w_ref, b_ref, o_ref):
    s = s_ref[...]
    s = s * _sigmoid(s)
    o_ref[...] = jnp.dot(s, w_ref[...], precision=HIGHEST, preferred_element_type=F32) + b_ref[...]


def _ada(cond, ada_w, ada_b):
    depth = ada_w.shape[0]
    n6 = ada_w.shape[2] // D_MODEL
    return pl.pallas_call(
        _ada_body,
        grid=(depth, n6),
        in_specs=[pl.BlockSpec((8, D_MODEL), lambda l, j: (0, 0)),
                  pl.BlockSpec((None, D_MODEL, D_MODEL), lambda l, j: (l, 0, j)),
                  pl.BlockSpec((None, 1, D_MODEL), lambda l, j: (l, 0, j))],
        out_specs=pl.BlockSpec((None, 8, D_MODEL), lambda l, j: (l, 0, j)),
        out_shape=jax.ShapeDtypeStruct((depth, 8, n6 * D_MODEL), F32),
        compiler_params=_params("parallel", "parallel"),
        name="ada_mod",
    )(cond, ada_w, ada_b.reshape(depth, 1, -1))


def _norm_mod(x, nw, shift, scale):
    ms = jnp.mean(x * x, axis=-1, keepdims=True)
    return (x * lax.rsqrt(ms + EPS) * nw) * (1.0 + scale) + shift


def _inproj_body(x_ref, xp_ref, xn_ref, mod_ref, nw_ref, w_ref, caw_ref, cab_ref, scw_ref, scb_ref,
                 p_ref, *, tm, nt):
    i = pl.program_id(1)
    xe = jnp.concatenate([xp_ref[...], x_ref[...], xn_ref[...]], axis=0)
    h = _norm_mod(xe, nw_ref[...], mod_ref[0:1, :], mod_ref[1:2, :])
    pr = jnp.dot(h.astype(BF16), w_ref[...], preferred_element_type=F32)
    rows = lax.broadcasted_iota(I32, (tm + 16, 1), 0)
    lo = jnp.where(i > 0, 0, 8)
    hi = jnp.where(i < nt - 1, tm + 16, tm + 8)
    vm = jnp.where((rows >= lo) & (rows < hi), 1.0, 0.0)

    def conv3(u, w, b):
        u = u * vm
        return u[7:tm + 7] * w[0:1, :] + u[8:tm + 8] * w[1:2, :] + u[9:tm + 9] * w[2:3, :] + b[...]

    ca = conv3(pr[:, SLAB:2 * SLAB] * pr[:, 2 * SLAB:3 * SLAB], caw_ref, cab_ref)
    p_ref[S_YA] = pr[8:tm + 8, 0:SLAB] * ca
    for s in range(7):
        p_ref[S_Q + s] = pr[8:tm + 8, (3 + s) * SLAB:(4 + s) * SLAB]
    cs = conv3(pr[:, 10 * SLAB:12 * SLAB], scw_ref, scb_ref)
    cs = cs * _sigmoid(cs)
    p_ref[S_XS] = cs[:, 0:SLAB]
    p_ref[S_BC] = cs[:, SLAB:2 * SLAB]
    p_ref[S_DTF] = pr[8:tm + 8, 12 * SLAB:13 * SLAB]
    p_ref[S_DTB] = pr[8:tm + 8, 13 * SLAB:14 * SLAB]


def _inproj(x, mod, nw, w_ext, caw, cab, scw, scb):
    bsz, t_len, _ = x.shape
    tm = min(512, t_len)
    nt = t_len // tm
    t8 = tm // 8
    full = lambda shape: pl.BlockSpec(shape, lambda b, i: (0,) * len(shape))
    return pl.pallas_call(
        functools.partial(_inproj_body, tm=tm, nt=nt),
        grid=(bsz, nt),
        in_specs=[pl.BlockSpec((None, tm, D_MODEL), lambda b, i: (b, i, 0)),
                  pl.BlockSpec((None, 8, D_MODEL), lambda b, i: (b, jnp.maximum(i * t8 - 1, 0), 0)),
                  pl.BlockSpec((None, 8, D_MODEL), lambda b, i: (b, jnp.minimum((i + 1) * t8, t_len // 8 - 1), 0)),
                  pl.BlockSpec((None, 8, D_MODEL), lambda b, i: (b, 0, 0)),
                  full((1, D_MODEL)), full(w_ext.shape), full(caw.shape), full(cab.shape),
                  full(scw.shape), full(scb.shape)],
        out_specs=pl.BlockSpec((N_SLABS, None, tm, SLAB), lambda b, i: (0, b, i, 0)),
        out_shape=jax.ShapeDtypeStruct((N_SLABS, bsz, t_len, SLAB), F32),
        compiler_params=_params("parallel", "parallel"),
        name="in_proj",
    )(x, x, x, mod, nw, w_ext, caw, cab, scw, scb)


def _scan_constants():
    t = np.arange(CHUNK)[:, None]
    r = np.arange(CHUNK)[None, :]
    out = {}
    for name, fwd in (("f", True), ("b", False)):
        blocks = [(r <= t) if fwd else (r >= t), (r > t) if fwd else (r < t)]
        qs, ks, masks = [], [], []
        for lvl in range(N_LEVELS):
            m = 1 << lvl
            blk = t // (2 * m)
            ref = blk * 2 * m + m - 1
            upper = (t % (2 * m)) >= m
            s_blk = (r // (2 * m))
            s_upper = (r % (2 * m)) >= m
            if fwd:
                qs.append(upper & (r > ref) & (r <= t))
                ks.append((~upper) & (r > t) & (r <= ref))
                masks.append(upper & (~s_upper) & (s_blk == blk))
            else:
                qs.append((~upper) & (r >= t) & (r <= ref))
                ks.append(upper & (r > ref) & (r < t))
                masks.append((~upper) & s_upper & (s_blk == blk))
        out["cm_" + name] = np.concatenate(blocks + [q | k for q, k in zip(qs, ks)], axis=0).astype(np.float32)
        out["lm_" + name] = np.stack([np.tile(mk, (1, N_HEADS)) for mk in masks]).astype(np.float32)
        out["su_" + name] = np.tile((r.T > r) if fwd else (r.T < r), (1, N_HEADS)).astype(np.float32)
        out["sc_" + name] = np.tile((t >= r) if fwd else (r >= t), (1, N_HEADS)).astype(np.float32)
    lane_head = np.arange(SLAB) // HEAD
    out["ebd"] = (lane_head[:, None] == lane_head[None, :]).astype(np.float32)
    lane_group = np.arange(2 * HEAD) // HEAD
    out["gm4"] = (lane_head[:, None] // 2 == lane_group[None, :]).astype(np.float32)
    return out


def _split_bf16(a):
    hi = a.astype(BF16)
    return hi, (a - hi.astype(F32)).astype(BF16)


def _head_blocks(a, mask):
    return jnp.concatenate([a.astype(BF16)] * N_HEADS, axis=0) * mask


def _hgrn_chunk(qv, kv, vv, logf, s_ref, k, cm, lm_ref, ebd, fwd):
    c = CHUNK
    e = jnp.exp(jnp.dot(cm, jnp.concatenate(_split_bf16(logf), axis=0), preferred_element_type=F32))
    eb = e.astype(BF16)
    qb, kb = qv.astype(BF16), kv.astype(BF16)
    p = jnp.zeros((c, SLAB), F32)
    for lvl in range(N_LEVELS):
        el = eb[(2 + lvl) * c:(3 + lvl) * c]
        p = p + _nt(qb * el, _head_blocks(kb * el, ebd)) * lm_ref[lvl]
    state = s_ref[k]
    o = jnp.dot((qv * kv).astype(BF16), ebd, preferred_element_type=F32) * vv
    o = o + jnp.dot(p.astype(BF16), _head_blocks(vv, ebd), preferred_element_type=F32)
    o = o + _nt((qv * e[0:c]).astype(BF16), state.astype(BF16))
    upd = _tn(vv.astype(BF16), (kv * e[c:2 * c]).astype(BF16))
    tot = e[c - 1:c] if fwd else e[0:1]
    s_ref[k] = state * tot + upd * ebd.astype(F32)
    return o


def _ssd_chunk(bc, xv, dt, a, s_ref, k, sm, um, caus, gm4, gms, ebd, fwd):
    c = CHUNK
    bv, cv = bc[:, :2 * HEAD], bc[:, 2 * HEAD:].astype(BF16)
    a_hi, a_lo = _split_bf16(a)
    ex = jnp.dot(sm, jnp.concatenate([a_hi, a_lo], axis=0), preferred_element_type=F32)
    cum, aft = ex[0:c], ex[c:]
    between = jnp.dot(sm[0:c], jnp.concatenate([a_hi * um, a_lo * um], axis=0),
                      preferred_element_type=F32)
    p = _nt(cv, _head_blocks(bv, gm4)) * (jnp.exp(between) * caus)
    xt = xv * dt
    state = s_ref[k]
    o = jnp.dot(p.astype(BF16), _head_blocks(xt, ebd), preferred_element_type=F32)
    o = o + jnp.dot(cv, state.astype(BF16), preferred_element_type=F32) * jnp.exp(cum)
    upd = _tn(bv.astype(BF16), (xt * jnp.exp(aft)).astype(BF16))
    tot = jnp.exp(cum[c - 1:c] if fwd else cum[0:1])
    s_ref[k] = state * tot + upd * gms
    return o


def _scan_body(qf, ff, vf, xsf, bcf, dtf, qb, fb, vb, xsb, bcb, dtb,
               lb_ref, alog_ref, dtbias_ref, cmf_ref, cmb_ref, lmf_ref, lmb_ref, suf_ref, sub_ref, scf_ref, scb_ref,
               gm4_ref, gms_ref, ebd_ref, h0_ref, d0_ref,
               ohf, ohb, osf, osb, hout_ref, dout_ref, sh_ref, sd_ref, *, layer, nck, nsteps):
    i = pl.program_id(1)

    @pl.when(i == 0)
    def _():
        sh_ref[...] = h0_ref[...]
        sd_ref[...] = d0_ref[...]

    ebd = ebd_ref[...]
    cmf, cmb = cmf_ref[...], cmb_ref[...]
    smf, smb = cmf_ref[0:2 * CHUNK, :], cmb_ref[0:2 * CHUNK, :]

    def lower_bound(d):
        rows = lb_ref[d]
        ex = jnp.exp(rows - jnp.max(rows, axis=0, keepdims=True))
        prob = ex / jnp.sum(ex, axis=0, keepdims=True)
        lb = jnp.zeros((1, SLAB), F32)
        for j in range(1, layer + 1):
            lb = lb + prob[j:j + 1]
        return lb

    lbs = [lower_bound(0), lower_bound(1)]

    def hgrn_inputs(q_raw, f_raw, lb):
        q = q_raw * _sigmoid(q_raw)
        log_sig = jnp.minimum(f_raw, 0.0) - jnp.log1p(jnp.exp(-jnp.abs(f_raw)))
        a = jnp.log(lb)
        b = jnp.log1p(-lb) + log_sig
        logf = jnp.maximum(a, b) + jnp.log1p(jnp.exp(-jnp.abs(a - b)))
        kk = (1.0 - lb) * _sigmoid(-f_raw)
        return q, kk, logf

    def ssd_inputs(dt_raw, d):
        dt = _softplus(dt_raw + dtbias_ref[d:d + 1, :])
        return dt, -jnp.exp(alog_ref[d:d + 1, :]) * dt

    def body(j, carry):
        rf = pl.ds(pl.multiple_of(j * CHUNK, CHUNK), CHUNK)
        rb = pl.ds(pl.multiple_of((nck - 1 - j) * CHUNK, CHUNK), CHUNK)
        q, kk, logf = hgrn_inputs(qf[rf, :], ff[rf, :], lbs[0])
        ohf[rf, :] = _hgrn_chunk(q, kk, vf[rf, :], logf, sh_ref, 0, cmf, lmf_ref, ebd, True)
        q, kk, logf = hgrn_inputs(qb[rb, :], fb[rb, :], lbs[1])
        ohb[rb, :] = _hgrn_chunk(q, kk, vb[rb, :], logf, sh_ref, 1, cmb, lmb_ref, ebd, False)
        dt, a = ssd_inputs(dtf[rf, :], 0)
        osf[rf, :] = _ssd_chunk(bcf[rf, :], xsf[rf, :], dt, a, sd_ref, 0, smf, suf_ref[...], scf_ref[...],
                                gm4_ref[...], gms_ref[...], ebd, True)
        dt, a = ssd_inputs(dtb[rb, :], 1)
        osb[rb, :] = _ssd_chunk(bcb[rb, :], xsb[rb, :], dt, a, sd_ref, 1, smb, sub_ref[...], scb_ref[...],
                                gm4_ref[...], gms_ref[...], ebd, False)
        return carry

    lax.fori_loop(0, nck, body, 0, unroll=2)

    @pl.when(i == nsteps - 1)
    def _():
        hout_ref[...] = sh_ref[...]
        dout_ref[...] = sd_ref[...]


def _scan(p3, hgrn_lb, alog_lane, dtbias_lane, s0, consts, layer):
    _, bsz, t_len, _ = p3.shape
    cb = min(256, t_len)
    nb = t_len // cb
    slab_f = lambda s: pl.BlockSpec((None, None, cb, SLAB), lambda b, i: (s, b, i, 0))
    slab_b = lambda s: pl.BlockSpec((None, None, cb, SLAB), lambda b, i: (s, b, nb - 1 - i, 0))
    full = lambda a: pl.BlockSpec(a.shape, lambda b, i: (0,) * a.ndim)
    bf = lambda name: consts[name].astype(BF16)
    cm2 = lambda name: jnp.concatenate([bf(name), bf(name)], axis=1)
    small = [hgrn_lb, alog_lane, dtbias_lane, cm2("cm_f"), cm2("cm_b"), consts["lm_f"], consts["lm_b"],
             bf("su_f"), bf("su_b"), consts["sc_f"], consts["sc_b"], bf("gm4"), consts["gm4"].T, bf("ebd")]
    o_f = pl.BlockSpec((None, cb, SLAB), lambda b, i: (b, i, 0))
    o_b = pl.BlockSpec((None, cb, SLAB), lambda b, i: (b, nb - 1 - i, 0))
    st_h = pl.BlockSpec((None, 2, SLAB, SLAB), lambda b, i: (b, 0, 0, 0))
    st_d = pl.BlockSpec((None, 2, 2 * HEAD, SLAB), lambda b, i: (b, 0, 0, 0))
    o_shape = jax.ShapeDtypeStruct((bsz, t_len, SLAB), F32)
    h0, d0 = s0
    return pl.pallas_call(
        functools.partial(_scan_body, layer=layer, nck=cb // CHUNK, nsteps=nb),
        grid=(bsz, nb),
        in_specs=([slab_f(s) for s in (S_Q, S_FF, S_I, S_XS, S_BC, S_DTF)]
                  + [slab_b(s) for s in (S_Q, S_FB, S_I, S_XS, S_BC, S_DTB)]
                  + [full(a) for a in small] + [st_h, st_d]),
        out_specs=[o_f, o_b, o_f, o_b, st_h, st_d],
        out_shape=[o_shape, o_shape, o_shape, o_shape, jax.ShapeDtypeStruct(h0.shape, F32),
                   jax.ShapeDtypeStruct(d0.shape, F32)],
        scratch_shapes=[pltpu.VMEM((2, SLAB, SLAB), F32), pltpu.VMEM((2, 2 * HEAD, SLAB), F32)],
        compiler_params=_params("parallel", "arbitrary"),
        name="chunk_scan",
    )(*([p3] * 12), *small, h0, d0)


def _fft_sizes(t_len):
    tf = 64 if t_len >= 4096 else 16
    return tf, t_len // tf


def _fft_constants(t_len):
    tf, ts = _fft_sizes(t_len)
    c = np.arange(SLAB)
    same = (c[:, None] // C_GDIM) == (c[None, :] // C_GDIM)
    ang = 2 * np.pi * ((c[:, None] % C_GDIM) * (c[None, :] % C_GDIM) % C_GDIM) / C_GDIM
    chan = np.concatenate([np.cos(ang) * same, -np.sin(ang) * same], axis=1)
    k2 = np.arange(ts)
    a = 2 * np.pi * (k2[:, None] * k2[None, :] % ts) / ts
    flat = np.block([[np.cos(a), np.sin(a)], [-np.sin(a), np.cos(a)]])
    k1 = np.arange(tf)[None, :, None]
    tfi = np.arange(tf)[None, None, :]
    kk2 = np.arange(ts)[:, None, None]
    th = 2 * np.pi * (((tfi * k1 * ts) + tfi * kk2) % t_len) / t_len
    scale = 1.0 / np.sqrt(t_len * C_GDIM)
    slab = np.concatenate([np.cos(th), np.sin(th)], axis=2) * scale
    return chan.astype(np.float32), flat.astype(np.float32), slab.astype(np.float32)


def _fft_flat_body(x_ref, chan_ref, flat_ref, y_ref, *, ts, groups):
    for g in range(groups):
        xg = x_ref[:, g * SLAB:(g + 1) * SLAB]
        a = jnp.dot(xg.astype(BF16), chan_ref[...], preferred_element_type=F32)
        z = jnp.concatenate([a[:, :SLAB], a[:, SLAB:]], axis=0).astype(BF16)
        y = jnp.dot(flat_ref[...], z, preferred_element_type=F32)
        y_ref[0, :, g * SLAB:(g + 1) * SLAB] = y[:ts]
        y_ref[1, :, g * SLAB:(g + 1) * SLAB] = y[ts:]


def _fft_slab_body(y_ref, m_ref, o_ref, *, kb):
    for j in range(kb):
        z = jnp.concatenate([y_ref[0, j], y_ref[1, j]], axis=0).astype(BF16)
        o_ref[j] = jnp.dot(m_ref[j], z, preferred_element_type=F32)


def _fourier(p3, consts):
    _, bsz, t_len, _ = p3.shape
    tf, ts = _fft_sizes(t_len)
    chan, flat, slab = (a.astype(BF16) for a in consts)
    groups = min(8, tf)
    lanes = groups * SLAB
    y = pl.pallas_call(
        functools.partial(_fft_flat_body, ts=ts, groups=groups),
        grid=(bsz, tf // groups),
        in_specs=[pl.BlockSpec((None, ts, lanes), lambda b, j: (b, 0, j)),
                  pl.BlockSpec(chan.shape, lambda b, j: (0, 0)),
                  pl.BlockSpec(flat.shape, lambda b, j: (0, 0))],
        out_specs=pl.BlockSpec((None, 2, ts, lanes), lambda b, j: (b, 0, 0, j)),
        out_shape=jax.ShapeDtypeStruct((bsz, 2, ts, tf * SLAB), F32),
        compiler_params=_params("parallel", "parallel"),
        name="fft_flat",
    )(p3[S_FOUR].reshape(bsz, ts, tf * SLAB), chan, flat)
    kb = min(8, ts)
    out = pl.pallas_call(
        functools.partial(_fft_slab_body, kb=kb),
        grid=(bsz, ts // kb),
        in_specs=[pl.BlockSpec((None, 2, kb, tf, SLAB), lambda b, j: (b, 0, j, 0, 0)),
                  pl.BlockSpec((kb, tf, 2 * tf), lambda b, j: (j, 0, 0))],
        out_specs=pl.BlockSpec((None, kb, tf, SLAB), lambda b, j: (b, j, 0, 0)),
        out_shape=jax.ShapeDtypeStruct((bsz, ts, tf, SLAB), F32),
        compiler_params=_params("parallel", "parallel"),
        name="fft_slab",
    )(y.reshape(bsz, 2, ts, tf, SLAB), slab)
    return out.transpose(0, 2, 1, 3).reshape(bsz, t_len, SLAB)


def _merge_body(x_ref, ya_ref, g_ref, z_ref, xs_ref, hf_ref, hb_ref, sf_ref, sb_ref, yc_ref,
                mod_ref, nmix_ref, nffn_ref, wg_ref, wb_ref, wo_ref, hnw_ref, sd_ref, snw_ref,
                e64_ref, e128_ref, wr_ref, br_ref, f_prev_ref,
                xo_ref, f_ref, ids_ref, wts_ref):
    del f_prev_ref
    x = x_ref[...]
    h = _norm_mod(x, nmix_ref[...], mod_ref[0:1, :], mod_ref[1:2, :]).astype(BF16)

    def group_rms(v, e_ref, width):
        ms = jnp.dot((v * v).astype(BF16), e_ref[...], preferred_element_type=F32) * (1.0 / width)
        return v * lax.rsqrt(ms + EPS)

    g = g_ref[...]
    y_b = group_rms(hf_ref[...] + hb_ref[...], e64_ref, HEAD) * hnw_ref[...] * (g * _sigmoid(g))
    z = z_ref[...]
    y_d = (sf_ref[...] + sb_ref[...] + sd_ref[...] * xs_ref[...]) * (z * _sigmoid(z))
    y_d = group_rms(y_d, e128_ref, 2 * HEAD) * snw_ref[...]
    branches = (ya_ref[...], y_b, yc_ref[...], y_d)
    merged = None
    for kb in range(N_BRANCH):
        gate = _sigmoid(jnp.dot(h, wg_ref[:, kb * D_MODEL:(kb + 1) * D_MODEL], preferred_element_type=F32))
        term = gate * jnp.dot(branches[kb].astype(BF16), wb_ref[kb], preferred_element_type=F32)
        merged = term if merged is None else merged + term
    mix = jnp.dot(merged.astype(BF16), wo_ref[...], preferred_element_type=F32)
    xo = x + mod_ref[2:3, :] * mix
    xo_ref[...] = xo
    f = _norm_mod(xo, nffn_ref[...], mod_ref[3:4, :], mod_ref[4:5, :])
    f_ref[...] = f

    lg = _nt(wr_ref[...], f, precision=HIGHEST) + br_ref[...]
    gl = lg[0:N_GROUPS]
    gidx = lax.broadcasted_iota(I32, gl.shape, 0)
    gmax = jnp.max(gl, axis=0, keepdims=True)
    g_top = jnp.min(jnp.where(gl == gmax, gidx, N_GROUPS), axis=0, keepdims=True)
    p_group = 1.0 / jnp.sum(jnp.exp(gl - gmax), axis=0, keepdims=True)
    e_in = jnp.zeros((EXPERTS_PER_GROUP, gl.shape[1]), F32)
    for grp in range(N_GROUPS):
        e_in = jnp.where(g_top == grp, lg[8 + 8 * grp:16 + 8 * grp], e_in)
    eidx = lax.broadcasted_iota(I32, e_in.shape, 0)
    v1 = jnp.max(e_in, axis=0, keepdims=True)
    i1 = jnp.min(jnp.where(e_in == v1, eidx, EXPERTS_PER_GROUP), axis=0, keepdims=True)
    rest = jnp.where(eidx == i1, -jnp.inf, e_in)
    v2 = jnp.max(rest, axis=0, keepdims=True)
    i2 = jnp.min(jnp.where(rest == v2, eidx, EXPERTS_PER_GROUP), axis=0, keepdims=True)
    w1 = 1.0 / (1.0 + jnp.exp(v2 - v1))
    row = lax.broadcasted_iota(I32, (8, gl.shape[1]), 0)
    base = g_top * EXPERTS_PER_GROUP
    ids_ref[...] = jnp.where(row == 0, base + i1, jnp.where(row == 1, base + i2, 0))
    wts_ref[...] = jnp.where(row == 0, p_group * w1, jnp.where(row == 1, p_group * (1.0 - w1), 0.0))


def _merge_body_first(*refs):
    n_in = len(refs) - 4
    return _merge_body(*refs[:n_in], None, *refs[n_in:])


def _merge(x, p3, scans, y_c, mod, nmix, nffn, wg, wb, wo, hnw, sd, snw, e64, e128, wr, br, f_rows, f_prev, row0):
    bsz, t_len, _ = x.shape
    tm = min(512, t_len)
    nt = t_len // tm
    r0 = row0 // tm
    row = lambda w: pl.BlockSpec((None, tm, w), lambda b, i: (b, i, 0))
    slab = lambda s: pl.BlockSpec((None, None, tm, SLAB), lambda b, i: (s, b, i, 0))
    full = lambda a: pl.BlockSpec(a.shape, lambda b, i: (0,) * a.ndim, pipeline_mode=pl.Buffered(1))
    consts = [nmix, nffn, wg, wb, wo, hnw, sd, snw, e64, e128, wr, br]
    lane_out = pl.BlockSpec((8, tm), lambda b, i: (0, b * nt + i))
    in_specs = ([row(D_MODEL)] + [slab(s) for s in (S_YA, S_G, S_Z, S_XS)] + [row(SLAB)] * 5
                + [pl.BlockSpec((None, 8, D_MODEL), lambda b, i: (b, 0, 0))] + [full(a) for a in consts])
    operands = [x, p3, p3, p3, p3, *scans, y_c, mod, *consts]
    aliases = {}
    if f_prev is not None:
        in_specs.append(pl.BlockSpec(memory_space=pl.ANY))
        aliases = {len(operands): 1}
        operands.append(f_prev)
    return pl.pallas_call(
        _merge_body if f_prev is not None else _merge_body_first,
        grid=(bsz, nt),
        in_specs=in_specs,
        out_specs=[row(D_MODEL), pl.BlockSpec((tm, D_MODEL), lambda b, i: (r0 + b * nt + i, 0)), lane_out, lane_out],
        out_shape=[jax.ShapeDtypeStruct(x.shape, F32), jax.ShapeDtypeStruct((f_rows, D_MODEL), F32),
                   jax.ShapeDtypeStruct((8, bsz * t_len), I32), jax.ShapeDtypeStruct((8, bsz * t_len), F32)],
        input_output_aliases=aliases,
        compiler_params=_params("parallel", "parallel"),
        name="merge_router",
    )(*operands)


def _rank_body(ids_ref, tri_ref, rank_ref, cnt_ref, carry_ref, *, nsteps):
    i = pl.program_id(0)

    @pl.when(i == 0)
    def _():
        carry_ref[...] = jnp.zeros_like(carry_ref)

    tr = ids_ref.shape[1]
    eidx = lax.broadcasted_iota(I32, (N_EXPERTS, tr), 0)
    oh0 = jnp.where(eidx == ids_ref[0:1, :], 1.0, 0.0)
    oh1 = jnp.where(eidx == ids_ref[1:2, :], 1.0, 0.0)
    both = oh0 + oh1
    before = jnp.dot(both.astype(BF16), tri_ref[...], preferred_element_type=F32)
    pos = before + carry_ref[...]
    r0 = jnp.sum(oh0 * pos, axis=0, keepdims=True)
    r1 = jnp.sum(oh1 * pos, axis=0, keepdims=True)
    row = lax.broadcasted_iota(I32, (8, tr), 0)
    rank_ref[...] = jnp.where(row == 0, r0, jnp.where(row == 1, r1, 0.0)).astype(I32)
    carry_ref[...] = carry_ref[...] + jnp.sum(both, axis=1, keepdims=True)

    @pl.when(i == nsteps - 1)
    def _():
        cnt_ref[...] = carry_ref[...].astype(I32)


def _moe_rank(ids):
    n = ids.shape[1]
    tr = _tile(n, 1024)
    tri = jnp.triu(jnp.ones((tr, tr), BF16), k=1)
    return pl.pallas_call(
        functools.partial(_rank_body, nsteps=n // tr),
        grid=(n // tr,),
        in_specs=[pl.BlockSpec((8, tr), lambda i: (0, i)), pl.BlockSpec((tr, tr), lambda i: (0, 0))],
        out_specs=[pl.BlockSpec((8, tr), lambda i: (0, i)), pl.BlockSpec((N_EXPERTS, 1), lambda i: (0, 0))],
        out_shape=[jax.ShapeDtypeStruct((8, n), I32), jax.ShapeDtypeStruct((N_EXPERTS, 1), I32)],
        scratch_shapes=[pltpu.VMEM((N_EXPERTS, 1), F32)],
        compiler_params=_params("arbitrary"),
        name="moe_rank",
    )(ids, tri)


def _ffn_body(be_ref, nu_ref, valid_ref, x_ref, w1_ref, w3_ref, w2_ref, o_ref, w1b_ref, w3b_ref, w2b_ref):
    i = pl.program_id(0)
    used = i < nu_ref[0]
    fresh = (i == 0) | (be_ref[i] != be_ref[jnp.maximum(i - 1, 0)])

    @pl.when(used & fresh)
    def _():
        w1b_ref[...] = w1_ref[...].astype(BF16)
        w3b_ref[...] = w3_ref[...].astype(BF16)
        w2b_ref[...] = w2_ref[...].astype(BF16)

    @pl.when(used)
    def _():
        rows = lax.broadcasted_iota(I32, (MOE_ROWS, 1), 0)
        xb = jnp.where(rows < valid_ref[i], x_ref[...], 0.0).astype(BF16)
        a = jnp.dot(xb, w1b_ref[...], preferred_element_type=F32)
        b = jnp.dot(xb, w3b_ref[...], preferred_element_type=F32)
        hid = (a * _sigmoid(a) * b).astype(BF16)
        o_ref[...] = jnp.dot(hid, w2b_ref[...], preferred_element_type=F32)

    @pl.when(jnp.logical_not(used))
    def _():
        o_ref[...] = jnp.zeros_like(o_ref)


def _moe_ffn(buf, block_expert, n_used, valid, w1, w3, w2, layer):
    nblk = buf.shape[0] // MOE_ROWS
    wspec = lambda shape: pl.BlockSpec((None, None) + shape, lambda i, be, nu, vr: (layer, be[i], 0, 0))
    return pl.pallas_call(
        _ffn_body,
        grid_spec=pltpu.PrefetchScalarGridSpec(
            num_scalar_prefetch=3,
            grid=(nblk,),
            in_specs=[pl.BlockSpec((MOE_ROWS, D_MODEL), lambda i, be, nu, vr: (i, 0)),
                      wspec((D_MODEL, D_FF)), wspec((D_MODEL, D_FF)), wspec((D_FF, D_MODEL))],
            out_specs=pl.BlockSpec((MOE_ROWS, D_MODEL), lambda i, be, nu, vr: (i, 0)),
            scratch_shapes=[pltpu.VMEM((D_MODEL, D_FF), BF16), pltpu.VMEM((D_MODEL, D_FF), BF16),
                            pltpu.VMEM((D_FF, D_MODEL), BF16)]),
        out_shape=jax.ShapeDtypeStruct(buf.shape, F32),
        compiler_params=_params("arbitrary"),
        name="moe_ffn",
    )(block_expert, n_used, valid, buf, w1, w3, w2)


def _sc_gather_rows(src, idx):
    m, width = idx.shape[0], src.shape[1]
    mesh = plsc.VectorSubcoreMesh(core_axis_name="c", subcore_axis_name="s",
                                  num_cores=SC_CORES, num_subcores=SC_SUBCORES)

    @functools.partial(pl.kernel, out_type=jax.ShapeDtypeStruct((m, width), src.dtype), mesh=mesh,
                       scratch_types=[], name="sc_gather_rows",
                       compiler_params=pltpu.CompilerParams(use_tc_tiling_on_sc=True))
    def gather(src_hbm, idx_hbm, out_hbm):
        def body(idx_vmem, out_vmem):
            pltpu.sync_copy(src_hbm.at[idx_vmem.at[0]], out_vmem)

        pltpu.emit_pipeline(
            body,
            grid=(m // SC_WINDOW,),
            in_specs=[pl.BlockSpec((None, 1, SC_WINDOW), lambda i: (i, 0, 0))],
            out_specs=[pl.BlockSpec((SC_WINDOW, width), lambda i: (i, 0))],
            core_axis_name=("c", "s"),
            dimension_semantics=(pltpu.PARALLEL,),
        )(idx_hbm, out_hbm)

    return gather(src, idx.reshape(m // SC_WINDOW, 1, SC_WINDOW))


def _sc_scatter_rows(x, idx, n_out):
    n, width = x.shape
    mesh = plsc.VectorSubcoreMesh(core_axis_name="c", subcore_axis_name="s",
                                  num_cores=SC_CORES, num_subcores=SC_SUBCORES)

    @functools.partial(pl.kernel, out_type=jax.ShapeDtypeStruct((n_out, width), x.dtype), mesh=mesh,
                       scratch_types=[], name="sc_scatter_rows",
                       compiler_params=pltpu.CompilerParams(use_tc_tiling_on_sc=True))
    def scatter(x_hbm, idx_hbm, out_hbm):
        def body(x_vmem, idx_vmem):
            for k in range(2):
                pltpu.sync_copy(x_vmem, out_hbm.at[idx_vmem.at[k]])

        pltpu.emit_pipeline(
            body,
            grid=(n // SC_WINDOW,),
            in_specs=[pl.BlockSpec((SC_WINDOW, width), lambda i: (i, 0)),
                      pl.BlockSpec((None, 2, SC_WINDOW), lambda i: (i, 0, 0))],
            out_specs=[],
            core_axis_name=("c", "s"),
            dimension_semantics=(pltpu.PARALLEL,),
        )(x_hbm, idx_hbm)

    return scatter(x, idx.reshape(2, n // SC_WINDOW, SC_WINDOW).transpose(1, 0, 2))


def _combine_dense_body(x_ref, g0_ref, g1_ref, wts_ref, mod_ref, fnw_ref, o_ref, *, tc, final):
    eye = jnp.where(lax.broadcasted_iota(I32, (tc, tc), 0) == lax.broadcasted_iota(I32, (tc, tc), 1), 1.0, 0.0)
    wcol = _nt(eye, wts_ref[...], precision=HIGHEST)
    y = wcol[:, 0:1] * g0_ref[...] + wcol[:, 1:2] * g1_ref[...]
    out = x_ref[...] + mod_ref[5:6, :] * y
    if final:
        ms = jnp.mean(out * out, axis=-1, keepdims=True)
        out = out * lax.rsqrt(ms + EPS) * fnw_ref[...]
    o_ref[...] = out


def _combine_dense(x, gathered, wts, mod, fnw, col0, final):
    bsz, t_len, _ = x.shape
    tc = min(256, t_len)
    nt = t_len // tc
    c0 = col0 // tc
    g_spec = lambda k: pl.BlockSpec((None, tc, D_MODEL), lambda b, i: (k, c0 + b * nt + i, 0))
    return pl.pallas_call(
        functools.partial(_combine_dense_body, tc=tc, final=final),
        grid=(bsz, nt),
        in_specs=[pl.BlockSpec((None, tc, D_MODEL), lambda b, i: (b, i, 0)), g_spec(0), g_spec(1),
                  pl.BlockSpec((8, tc), lambda b, i: (0, c0 + b * nt + i)),
                  pl.BlockSpec((None, 8, D_MODEL), lambda b, i: (b, 0, 0)),
                  pl.BlockSpec((1, D_MODEL), lambda b, i: (0, 0))],
        out_specs=pl.BlockSpec((None, tc, D_MODEL), lambda b, i: (b, i, 0)),
        out_shape=jax.ShapeDtypeStruct(x.shape, F32),
        compiler_params=_params("parallel", "parallel"),
        name="moe_combine_dense",
    )(x, gathered, gathered, wts, mod, fnw)


def _moe(f_all, ids, w1, w3, w2, layer):
    n = ids.shape[1]
    rank, counts = _moe_rank(ids)
    counts = counts[:, 0]
    padded = (counts + MOE_ROWS - 1) // MOE_ROWS * MOE_ROWS
    pad_ends = jnp.cumsum(padded)
    pad_starts = pad_ends - padded
    onehot = ids[None, :2] == jnp.arange(N_EXPERTS, dtype=I32)[:, None, None]
    dest = jnp.sum(jnp.where(onehot, pad_starts[:, None, None], 0), axis=0) + rank[:2]
    nblk = (2 * n) // MOE_ROWS + N_EXPERTS
    block_start = jnp.arange(nblk, dtype=I32) * MOE_ROWS
    block_expert = jnp.minimum(jnp.sum(block_start[:, None] >= pad_ends[None, :], axis=-1), N_EXPERTS - 1)
    n_used = (pad_ends[-1:] // MOE_ROWS).astype(I32)
    ends = jnp.sum(jnp.where(block_expert[None, :] == jnp.arange(N_EXPERTS, dtype=I32)[:, None],
                             (pad_starts + counts)[:, None], 0), axis=0)
    valid = jnp.clip(ends - block_start, 0, MOE_ROWS).astype(I32)
    buf = _sc_scatter_rows(f_all, dest, nblk * MOE_ROWS)
    y_sorted = _moe_ffn(buf, block_expert.astype(I32), n_used, valid, w1, w3, w2, layer)
    return y_sorted, dest


def kernel(x, c, ctx, c_ctx, ada_w, ada_b, norm_mix_w, norm_ffn_w, w_in, conv_a_w, conv_a_b, hgrn_lb,
           hgrn_norm_w, ssm_conv_w, ssm_conv_b, ssm_A_log, ssm_dt_bias, ssm_D, ssm_norm_w, w_branch, w_out,
           router_group_w, router_group_b, router_expert_w, router_expert_b, moe_w1, moe_w3, moe_w2,
           final_norm_w):
    depth = ada_w.shape[0]
    bsz, t_len, _ = x.shape
    t_ctx = ctx.shape[1]
    n_lat = bsz * t_len

    cond = jnp.concatenate([c, c_ctx[None, :], jnp.zeros((8 - bsz - 1, D_MODEL), F32)], axis=0)
    mods = _ada(cond, ada_w, ada_b)
    sc = {k: jnp.asarray(v) for k, v in _scan_constants().items()}
    fft_l = tuple(jnp.asarray(a) for a in _fft_constants(t_len))
    fft_c = tuple(jnp.asarray(a) for a in _fft_constants(t_ctx))
    e64 = sc["ebd"].astype(BF16)
    lane128 = np.arange(SLAB) // (2 * HEAD)
    e128 = jnp.asarray(lane128[:, None] == lane128[None, :], BF16)
    zero_state = (jnp.zeros((bsz, 2, SLAB, SLAB), F32), jnp.zeros((bsz, 2, 2 * HEAD, SLAB), F32))
    fnw = final_norm_w.reshape(1, D_MODEL)

    for l in range(depth):
        last = l == depth - 1
        six = mods[l].reshape(8, 6, D_MODEL)
        mod_l = jnp.pad(six[:bsz], ((0, 0), (0, 2), (0, 0)))
        mod_c = jnp.broadcast_to(jnp.pad(six[bsz], ((0, 2), (0, 0))), (bsz, 8, D_MODEL))

        wl = w_in[l]
        dt0 = 12 * SLAB
        w_dt = jnp.repeat(wl[:, dt0:dt0 + 2 * N_HEADS], HEAD, axis=1)
        w_ext = jnp.concatenate([wl[:, :dt0], w_dt], axis=1).astype(BF16)
        w_gate = wl[:, dt0 + 2 * N_HEADS:].astype(BF16)
        scw, scb = ssm_conv_w[l], ssm_conv_b[l][None, :]
        caw, cab = conv_a_w[l], conv_a_b[l][None, :]
        nmix, nffn = norm_mix_w[l][None, :], norm_ffn_w[l][None, :]
        alog_lane = jnp.repeat(ssm_A_log[l], HEAD, axis=1)
        dtbias_lane = jnp.repeat(ssm_dt_bias[l], HEAD, axis=1)
        hnw = jnp.tile(hgrn_norm_w[l], N_HEADS)[None, :]
        sd = jnp.repeat(ssm_D[l], HEAD)[None, :]
        snw = ssm_norm_w[l][None, :]
        wb, wo = w_branch[l].astype(BF16), w_out[l].astype(BF16)
        wr = jnp.concatenate([router_group_w[l].T, jnp.zeros((8 - N_GROUPS, D_MODEL), F32),
                              router_expert_w[l].T], axis=0)
        br = jnp.concatenate([router_group_b[l], jnp.zeros((8 - N_GROUPS,), F32),
                              router_expert_b[l]])[:, None]
        merge_w = (nmix, nffn, w_gate, wb, wo, hnw, sd, snw, e64, e128, wr, br)

        p3_c = _inproj(ctx, mod_c, nmix, w_ext, caw, cab, scw, scb)
        *scans_c, h_state, d_state = _scan(p3_c, hgrn_lb, alog_lane, dtbias_lane, zero_state, sc, l)
        p3_l = _inproj(x, mod_l, nmix, w_ext, caw, cab, scw, scb)
        *scans_l, _, _ = _scan(p3_l, hgrn_lb, alog_lane, dtbias_lane, (h_state, d_state), sc, l)
        yc_l = _fourier(p3_l, fft_l)
        f_rows = n_lat if last else n_lat + bsz * t_ctx
        x, f_all, ids, wts = _merge(x, p3_l, scans_l, yc_l, mod_l, *merge_w, f_rows, None, 0)
        if not last:
            yc_c = _fourier(p3_c, fft_c)
            ctx, f_all, ids_c, wts_c = _merge(ctx, p3_c, scans_c, yc_c, mod_c, *merge_w, f_rows, f_all, n_lat)
            ids = jnp.concatenate([ids, ids_c], axis=1)
            wts = jnp.concatenate([wts, wts_c], axis=1)
        y_sorted, dest = _moe(f_all, ids, moe_w1, moe_w3, moe_w2, l)
        gathered = _sc_gather_rows(y_sorted, dest.reshape(-1)).reshape(2, -1, D_MODEL)
        x = _combine_dense(x, gathered, wts, mod_l, fnw, 0, last)
        if not last:
            ctx = _combine_dense(ctx, gathered, wts, mod_c, fnw, n_lat, False)
    return x
```
